```python
import jax, jax.numpy as jnp
from jax import lax
import numpy as np

D_MODEL = 1024
BATCH = 8
SEQ = 2048
DEPTH = 4
DEC_BATCH = 128
DEC_SEQ = 1
PAST_LEN = 16384
PAGE_SIZE = 128

N_MIXERS = 3
N_A = (DEPTH + 2) // 3
N_B = (DEPTH + 1) // 3
N_C = DEPTH // 3
A_WIDTH = 3
B_WIDTH = 31
B_CH = D_MODEL
C_EXPAND = 128
C_HEADS = D_MODEL // C_EXPAND
C_DK = C_EXPAND
C_DV = D_MODEL // C_HEADS
C_CHUNK = 32
D_FF = -(-8 * D_MODEL // (3 * 256)) * 256
NORM_EPS = 1e-6
LN_EPS = 1e-5

kernel_name = "hybrid_shortconv_conformer_hgrn2_step"


def rms_norm(x, g):
    xf = x.astype(jnp.float32)
    y = xf * lax.rsqrt(jnp.mean(xf * xf, axis=-1, keepdims=True) + NORM_EPS)
    return (y * g.astype(jnp.float32)).astype(x.dtype)


def layer_norm(x, g, b):
    xf = x.astype(jnp.float32)
    mu = jnp.mean(xf, axis=-1, keepdims=True)
    var = jnp.mean(jnp.square(xf - mu), axis=-1, keepdims=True)
    y = (xf - mu) * lax.rsqrt(var + LN_EPS)
    return (y * g.astype(jnp.float32) + b.astype(jnp.float32)).astype(x.dtype)


def causal_dwconv(u, buf, w):
    width, ch = w.shape
    full = jnp.concatenate([buf.astype(u.dtype), u], axis=1)
    y = lax.conv_general_dilated(full, w.astype(u.dtype)[:, None, :], window_strides=(1,),
                                 padding='VALID', dimension_numbers=('NWC', 'WIO', 'NWC'),
                                 feature_group_count=ch)
    return y, full[:, -(width - 1):, :]


def short_conv_mixer(h, buf, w_in, conv_w, w_out):
    bg, cg, hv = jnp.split(h @ w_in, 3, axis=-1)
    y, new_buf = causal_dwconv(cg * hv, buf, conv_w)
    return (bg * y) @ w_out, new_buf


def conformer_conv_mixer(h, buf, w_pw1, b_pw1, dw_w, dw_b, ln_g, ln_b, w_pw2, b_pw2):
    a, gate = jnp.split(h @ w_pw1 + b_pw1, 2, axis=-1)
    u = a * jax.nn.sigmoid(gate)
    y, new_buf = causal_dwconv(u, buf, dw_w)
    y = layer_norm(y + dw_b, ln_g, ln_b)
    return jax.nn.silu(y) @ w_pw2 + b_pw2, new_buf


def hgrn2_chunked(q, k, v, logf, s0):
    n, nh, t, dk = q.shape
    dv = v.shape[-1]
    c = min(C_CHUNK, t)
    pad = (-t) % c
    if pad:
        pw = ((0, 0), (0, 0), (0, pad), (0, 0))
        q, k, v, logf = (jnp.pad(a, pw) for a in (q, k, v, logf))
    nc = (t + pad) // c

    def blocks(a):
        return jnp.moveaxis(a.reshape(n, nh, nc, c, a.shape[-1]), 2, 0)

    mask = jnp.tril(jnp.ones((c, c), dtype=bool))[:, :, None]

    def step(s, xs):
        qc, kc, vc, gc = xs
        b = jnp.cumsum(gc, axis=2)
        diff = b[:, :, :, None, :] - b[:, :, None, :, :]
        decay = jnp.where(mask, jnp.exp(jnp.where(mask, diff, 0.0)), 0.0)
        scores = jnp.einsum('nhtk,nhsk,nhtsk->nhts', qc, kc, decay)
        o = (jnp.einsum('nhtk,nhkv->nhtv', qc * jnp.exp(b), s)
             + jnp.einsum('nhts,nhsv->nhtv', scores, vc))
        b_last = b[:, :, -1:, :]
        s_new = (jnp.exp(b_last[:, :, 0, :])[..., None] * s
                 + jnp.einsum('nhsk,nhsv->nhkv', kc * jnp.exp(b_last - b), vc))
        return s_new, o

    s_fin, o = lax.scan(step, s0, (blocks(q), blocks(k), blocks(v), blocks(logf)))
    o = jnp.moveaxis(o, 0, 2).reshape(n, nh, nc * c, dv)[:, :, :t]
    return o, s_fin


def hgrn2_mixer(h, s0, lb, w_qfig, gnorm, w_out):
    n, t, _ = h.shape
    q, f, i, g = jnp.split(h @ w_qfig, 4, axis=-1)

    def heads(a, d):
        return a.astype(jnp.float32).reshape(n, t, C_HEADS, d).transpose(0, 2, 1, 3)

    lb = lb.reshape(C_HEADS, 1, C_DK)
    logf = jnp.logaddexp(jnp.log(lb), jnp.log1p(-lb) + jax.nn.log_sigmoid(heads(f, C_DK)))
    k = -jnp.expm1(logf)
    qh = jax.nn.silu(heads(q, C_DK)) * (C_DK ** -0.5)
    o, s_new = hgrn2_chunked(qh, k, heads(i, C_DV), logf, s0.astype(jnp.float32))
    o = o * lax.rsqrt(jnp.mean(o * o, axis=-1, keepdims=True) + NORM_EPS) * gnorm.astype(jnp.float32)
    o = o.transpose(0, 2, 1, 3).reshape(n, t, D_MODEL).astype(h.dtype)
    return (o * jax.nn.silu(g)) @ w_out, s_new.astype(s0.dtype)


def swiglu(h, w_gate_up, w_down):
    gate, up = jnp.split(h @ w_gate_up, 2, axis=-1)
    return (jax.nn.silu(gate) * up) @ w_down


def trunk(x, conva, convb, hgrn, p):
    sm = jax.nn.softmax(p['c_lower_bounds'].astype(jnp.float32), axis=0)
    lower = jnp.cumsum(sm, axis=0) - sm[0]
    new_a, new_b, new_c = [], [], []
    for i in range(DEPTH):
        kind, j = i % N_MIXERS, i // N_MIXERS
        h = rms_norm(x, p['norm_mix'][i])
        if kind == 0:
            m, s = short_conv_mixer(h, conva[j], p['a_w_in'][j], p['a_conv_w'][j], p['a_w_out'][j])
            new_a.append(s)
        elif kind == 1:
            m, s = conformer_conv_mixer(h, convb[j], p['b_w_pw1'][j], p['b_b_pw1'][j], p['b_dw_w'][j],
                                        p['b_dw_b'][j], p['b_ln_g'][j], p['b_ln_b'][j],
                                        p['b_w_pw2'][j], p['b_b_pw2'][j])
            new_b.append(s)
        else:
            m, s = hgrn2_mixer(h, hgrn[j], lower[i], p['c_w_qfig'][j], p['c_gnorm'][j], p['c_w_out'][j])
            new_c.append(s)
        x = x + m
        x = x + swiglu(rms_norm(x, p['norm_ffn'][i]), p['ffn_w_gate_up'][i], p['ffn_w_down'][i])
    return rms_norm(x, p['norm_final']), jnp.stack(new_a), jnp.stack(new_b), jnp.stack(new_c)


def setup_inputs(seed: int = 0) -> dict:
    key = jax.random.key(seed)
    ks = jax.random.split(key, 32)

    def nrm(k, shape, scale):
        return jax.random.normal(k, shape, jnp.float32) * scale

    out_scale = (2 * DEPTH) ** -0.5
    d = D_MODEL
    return {
        "x_prompt": nrm(ks[0], (BATCH, SEQ, d), 1.0),
        "x_sample": nrm(ks[1], (DEC_BATCH, DEC_SEQ, d), 1.0),
        "state_conva": nrm(ks[2], (N_A, DEC_BATCH, A_WIDTH - 1, d), 1.0),
        "state_convb": nrm(ks[3], (N_B, DEC_BATCH, B_WIDTH - 1, B_CH), 0.5),
        "state_hgrn": nrm(ks[4], (N_C, DEC_BATCH, C_HEADS, C_DK, C_DV), 1.0),
        "norm_mix": 1.0 + nrm(ks[5], (DEPTH, d), 0.02),
        "a_w_in": nrm(ks[6], (N_A, d, 3 * d), d ** -0.5),
        "a_conv_w": nrm(ks[7], (N_A, A_WIDTH, d), A_WIDTH ** -0.5),
        "a_w_out": nrm(ks[8], (N_A, d, d), d ** -0.5 * out_scale),
        "b_w_pw1": nrm(ks[9], (N_B, d, 2 * B_CH), d ** -0.5),
        "b_b_pw1": nrm(ks[10], (N_B, 2 * B_CH), 0.02),
        "b_dw_w": nrm(ks[11], (N_B, B_WIDTH, B_CH), B_WIDTH ** -0.5),
        "b_dw_b": nrm(ks[12], (N_B, B_CH), 0.02),
        "b_ln_g": 1.0 + nrm(ks[13], (N_B, B_CH), 0.02),
        "b_ln_b": nrm(ks[14], (N_B, B_CH), 0.02),
        "b_w_pw2": nrm(ks[15], (N_B, B_CH, d), B_CH ** -0.5 * out_scale),
        "b_b_pw2": nrm(ks[16], (N_B, d), 0.02),
        "c_lower_bounds": nrm(ks[17], (DEPTH, d), 0.1),
        "c_w_qfig": nrm(ks[18], (N_C, d, 4 * d), d ** -0.5),
        "c_gnorm": 1.0 + nrm(ks[19], (N_C, C_DV), 0.02),
        "c_w_out": nrm(ks[20], (N_C, d, d), d ** -0.5 * out_scale),
        "norm_ffn": 1.0 + nrm(ks[21], (DEPTH, d), 0.02),
        "ffn_w_gate_up": nrm(ks[22], (DEPTH, d, 2 * D_FF), d ** -0.5),
        "ffn_w_down": nrm(ks[23], (DEPTH, D_FF, d), D_FF ** -0.5 * out_scale),
        "norm_final": 1.0 + nrm(ks[24], (d,), 0.02),
    }


def reference(x_prompt, x_sample, state_conva, state_convb, state_hgrn, norm_mix, a_w_in, a_conv_w,
              a_w_out, b_w_pw1, b_b_pw1, b_dw_w, b_dw_b, b_ln_g, b_ln_b, b_w_pw2, b_b_pw2,
              c_lower_bounds, c_w_qfig, c_gnorm, c_w_out, norm_ffn, ffn_w_gate_up, ffn_w_down,
              norm_final):
    p = dict(norm_mix=norm_mix, a_w_in=a_w_in, a_conv_w=a_conv_w, a_w_out=a_w_out,
             b_w_pw1=b_w_pw1, b_b_pw1=b_b_pw1, b_dw_w=b_dw_w, b_dw_b=b_dw_b, b_ln_g=b_ln_g,
             b_ln_b=b_ln_b, b_w_pw2=b_w_pw2, b_b_pw2=b_b_pw2, c_lower_bounds=c_lower_bounds,
             c_w_qfig=c_w_qfig, c_gnorm=c_gnorm, c_w_out=c_w_out, norm_ffn=norm_ffn,
             ffn_w_gate_up=ffn_w_gate_up, ffn_w_down=ffn_w_down, norm_final=norm_final)
    nb = x_prompt.shape[0]
    za = jnp.zeros((N_A, nb) + state_conva.shape[2:], state_conva.dtype)
    zb = jnp.zeros((N_B, nb) + state_convb.shape[2:], state_convb.dtype)
    zc = jnp.zeros((N_C, nb) + state_hgrn.shape[2:], state_hgrn.dtype)
    y_prompt, conva_prompt, convb_prompt, hgrn_prompt = trunk(x_prompt, za, zb, zc, p)
    y_sample, conva_sample, convb_sample, hgrn_sample = trunk(x_sample, state_conva, state_convb,
                                                              state_hgrn, p)
    return (y_prompt, y_sample, conva_prompt, conva_sample, convb_prompt, convb_sample,
            hgrn_prompt, hgrn_sample)
```

```python
import functools

import numpy as np
import jax
import jax.numpy as jnp
from jax import lax
from jax.experimental import pallas as pl
from jax.experimental.pallas import tpu as pltpu

F32 = jnp.float32
BF16 = jnp.bfloat16

D_MODEL = 1024
DEPTH = 4
D_FF = 2816
A_WIDTH = 3
B_WIDTH = 31
C_HEADS = 8
C_DK = 128
C_DV = 128
NORM_EPS = 1e-6
LN_EPS = 1e-5

V7X_SUBLANES = 8
V7X_LANES = 128
V7X_VMEM_LIMIT_BYTES = 56 * 1024 * 1024

TOK_TILE = 512
FF_CHUNK = 256
HG_CHUNK = 128
HG_LEVELS = 7
SAMPLE_BLOCK_B = 16
SAMPLE_BLOCK_C = 8
A_HALO = V7X_SUBLANES
B_HALO = 32


def _params(*sem):
    return pltpu.CompilerParams(dimension_semantics=sem,
                                vmem_limit_bytes=V7X_VMEM_LIMIT_BYTES)


def _const_spec(shape):
    nd = len(shape)
    return pl.BlockSpec(shape, lambda *_: (0,) * nd, pipeline_mode=pl.Buffered(1))


def _rms(x, g):
    ms = jnp.mean(x * x, axis=-1, keepdims=True)
    return x * lax.rsqrt(ms + NORM_EPS) * g


def _silu(x):
    return x * jax.nn.sigmoid(x)


def _dot(a, b):
    return jnp.dot(a, b, preferred_element_type=F32)


def _dot_nt(a, b):
    return lax.dot_general(a, b, (((1,), (1,)), ((), ())), preferred_element_type=F32)


def _dot_tn(a, b):
    return lax.dot_general(a, b, (((0,), (0,)), ((), ())), preferred_element_type=F32)


def _ffn_kernel(x_ref, g_ref, wgu_ref, wd_ref, gf_ref, o_ref, h_ref, a_ref, *, final):
    x = x_ref[...]
    h_ref[...] = _rms(x, g_ref[...]).astype(BF16)
    for j in range(D_FF // FF_CHUNK):
        lo = j * FF_CHUNK
        gate = _dot(h_ref[...], wgu_ref[:, lo:lo + FF_CHUNK])
        up = _dot(h_ref[...], wgu_ref[:, D_FF + lo:D_FF + lo + FF_CHUNK])
        a_ref[:, lo:lo + FF_CHUNK] = (_silu(gate) * up).astype(BF16)
    y = x + _dot(a_ref[...], wd_ref[...])
    if final:
        y = _rms(y, gf_ref[...])
    o_ref[...] = y


def _ffn(x2d, g, wgu, wd, gf, *, tm, final):
    m = x2d.shape[0]
    return pl.pallas_call(
        functools.partial(_ffn_kernel, final=final),
        out_shape=jax.ShapeDtypeStruct((m, D_MODEL), F32),
        grid=(m // tm,),
        in_specs=[pl.BlockSpec((tm, D_MODEL), lambda i: (i, 0)),
                  _const_spec((1, D_MODEL)),
                  _const_spec((D_MODEL, 2 * D_FF)),
                  _const_spec((D_FF, D_MODEL)),
                  _const_spec((1, D_MODEL))],
        out_specs=pl.BlockSpec((tm, D_MODEL), lambda i: (i, 0)),
        scratch_shapes=[pltpu.VMEM((tm, D_MODEL), BF16),
                        pltpu.VMEM((tm, D_FF), BF16)],
        compiler_params=_params("arbitrary"),
        name="ffn",
    )(x2d, g, wgu, wd, gf)


def _carry_halo(ubuf_ref, halo, tt):
    j = pl.program_id(1)

    @pl.when(j == 0)
    def _():
        ubuf_ref[0:halo, :] = jnp.zeros((halo, D_MODEL), F32)

    @pl.when(j > 0)
    def _():
        ubuf_ref[0:halo, :] = ubuf_ref[tt:tt + halo, :]


def _mixa_kernel(x_ref, g_ref, win_ref, cw_ref, wout_ref, o_ref, st_ref, h_ref, ubuf_ref, *, tt):
    x = x_ref[...]
    h_ref[...] = _rms(x, g_ref[...]).astype(BF16)
    _carry_halo(ubuf_ref, A_HALO, tt)
    cg = _dot(h_ref[...], win_ref[:, D_MODEL:2 * D_MODEL])
    hv = _dot(h_ref[...], win_ref[:, 2 * D_MODEL:3 * D_MODEL])
    ubuf_ref[A_HALO:A_HALO + tt, :] = cg * hv
    y = cw_ref[A_WIDTH - 1:A_WIDTH, :] * ubuf_ref[A_HALO:A_HALO + tt, :]
    for k in range(A_WIDTH - 1):
        off = A_HALO - (A_WIDTH - 1) + k
        y = y + cw_ref[k:k + 1, :] * ubuf_ref[off:off + tt, :]
    bg = _dot(h_ref[...], win_ref[:, 0:D_MODEL])
    o_ref[...] = x + _dot((bg * y).astype(BF16), wout_ref[...])

    @pl.when(pl.program_id(1) == pl.num_programs(1) - 1)
    def _():
        st_ref[...] = ubuf_ref[A_HALO + tt - (A_WIDTH - 1):A_HALO + tt, :]


def _mixa_prompt(x, g, win, cw, wout):
    n, t, _ = x.shape
    tt = TOK_TILE
    return pl.pallas_call(
        functools.partial(_mixa_kernel, tt=tt),
        out_shape=(jax.ShapeDtypeStruct((n, t, D_MODEL), F32),
                   jax.ShapeDtypeStruct((n, A_WIDTH - 1, D_MODEL), F32)),
        grid=(n, t // tt),
        in_specs=[pl.BlockSpec((None, tt, D_MODEL), lambda i, j: (i, j, 0)),
                  _const_spec((1, D_MODEL)),
                  _const_spec((D_MODEL, 3 * D_MODEL)),
                  _const_spec((A_WIDTH, D_MODEL)),
                  _const_spec((D_MODEL, D_MODEL))],
        out_specs=(pl.BlockSpec((None, tt, D_MODEL), lambda i, j: (i, j, 0)),
                   pl.BlockSpec((None, A_WIDTH - 1, D_MODEL), lambda i, j: (i, 0, 0))),
        scratch_shapes=[pltpu.VMEM((tt, D_MODEL), BF16),
                        pltpu.VMEM((A_HALO + tt, D_MODEL), F32)],
        compiler_params=_params("arbitrary", "arbitrary"),
        name="mix_shortconv",
    )(x, g, win, cw, wout)


def _layer_norm(y, g, b):
    mu = jnp.mean(y, axis=-1, keepdims=True)
    yc = y - mu
    var = jnp.mean(yc * yc, axis=-1, keepdims=True)
    return yc * lax.rsqrt(var + LN_EPS) * g + b


def _mixb_kernel(x_ref, g_ref, w1_ref, b1_ref, dw_ref, dwb_ref, lng_ref, lnb_ref, w2_ref, b2_ref,
                 o_ref, st_ref, h_ref, ubuf_ref, *, tt):
    x = x_ref[...]
    h_ref[...] = _rms(x, g_ref[...]).astype(BF16)
    _carry_halo(ubuf_ref, B_HALO, tt)
    a = _dot(h_ref[...], w1_ref[:, 0:D_MODEL]) + b1_ref[:, 0:D_MODEL]
    gate = _dot(h_ref[...], w1_ref[:, D_MODEL:2 * D_MODEL]) + b1_ref[:, D_MODEL:2 * D_MODEL]
    ubuf_ref[B_HALO:B_HALO + tt, :] = a * jax.nn.sigmoid(gate)
    base = B_HALO - (B_WIDTH - 1)
    y = dw_ref[0:1, :] * ubuf_ref[base:base + tt, :]
    for k in range(1, B_WIDTH):
        y = y + dw_ref[k:k + 1, :] * ubuf_ref[base + k:base + k + tt, :]
    z = _silu(_layer_norm(y + dwb_ref[...], lng_ref[...], lnb_ref[...]))
    o_ref[...] = x + _dot(z.astype(BF16), w2_ref[...]) + b2_ref[...]

    @pl.when(pl.program_id(1) == pl.num_programs(1) - 1)
    def _():
        st_ref[...] = ubuf_ref[B_HALO + tt - (B_WIDTH - 1):B_HALO + tt, :]


def _mixb_prompt(x, g, w1, b1, dw, dwb, lng, lnb, w2, b2):
    n, t, _ = x.shape
    tt = TOK_TILE
    return pl.pallas_call(
        functools.partial(_mixb_kernel, tt=tt),
        out_shape=(jax.ShapeDtypeStruct((n, t, D_MODEL), F32),
                   jax.ShapeDtypeStruct((n, B_WIDTH - 1, D_MODEL), F32)),
        grid=(n, t // tt),
        in_specs=[pl.BlockSpec((None, tt, D_MODEL), lambda i, j: (i, j, 0)),
                  _const_spec((1, D_MODEL)),
                  _const_spec((D_MODEL, 2 * D_MODEL)),
                  _const_spec((1, 2 * D_MODEL)),
                  _const_spec((B_WIDTH, D_MODEL)),
                  _const_spec((1, D_MODEL)),
                  _const_spec((1, D_MODEL)),
                  _const_spec((1, D_MODEL)),
                  _const_spec((D_MODEL, D_MODEL)),
                  _const_spec((1, D_MODEL))],
        out_specs=(pl.BlockSpec((None, tt, D_MODEL), lambda i, j: (i, j, 0)),
                   pl.BlockSpec((None, B_WIDTH - 1, D_MODEL), lambda i, j: (i, 0, 0))),
        scratch_shapes=[pltpu.VMEM((tt, D_MODEL), BF16),
                        pltpu.VMEM((B_HALO + tt, D_MODEL), F32)],
        compiler_params=_params("arbitrary", "arbitrary"),
        name="mix_conformer",
    )(x, g, w1, b1, dw, dwb, lng, lnb, w2, b2)


def _hgrn_constants():
    c = HG_CHUNK
    t = np.arange(c)[:, None]
    r = np.arange(c)[None, :]
    blocks = [(r <= t), (r > t)]
    masks = [(t == r)]
    for lvl in range(HG_LEVELS):
        h = 1 << lvl
        m = (t // (2 * h)) * (2 * h) + h - 1
        upper = (t & h) != 0
        blocks.append(np.where(upper, (r > m) & (r <= t), (r > t) & (r <= m)))
        masks.append((((t ^ r) >> lvl) == 1) & upper)
    a_stack = np.concatenate(blocks, axis=0).astype(np.float32)
    m_stack = np.stack(masks, axis=0).astype(np.float32)
    return jnp.asarray(a_stack, BF16), jnp.asarray(m_stack, F32)


def _gates(fx, lb):
    sig = jax.nn.sigmoid(fx)
    return lb + (1.0 - lb) * sig, (1.0 - lb) * (1.0 - sig)


def _hgrn_chunk(qx, fx, v, s, lb, a_ref, m_ref):
    c = HG_CHUNK
    f, kk = _gates(fx, lb)
    logf = jnp.log(f)
    q = _silu(qx) * (C_DK ** -0.5)
    hi = logf.astype(BF16)
    lo = (logf - hi.astype(F32)).astype(BF16)
    parts = jnp.concatenate([hi, lo], axis=1)

    def seg_exp(r):
        d2 = _dot(a_ref[r * c:(r + 1) * c, :], parts)
        return jnp.exp(d2[:, 0:C_DK] + d2[:, C_DK:2 * C_DK])

    e_b = seg_exp(0)
    e_e = seg_exp(1)
    vb = v.astype(BF16)
    o = _dot((q * e_b).astype(BF16), s.astype(BF16))
    scores = _dot_nt(q.astype(BF16), kk.astype(BF16)) * m_ref[0]
    row = lax.broadcasted_iota(jnp.int32, (c, C_DK), 0)
    for lvl in range(HG_LEVELS):
        z = (jnp.where((row & (1 << lvl)) != 0, q, kk) * seg_exp(2 + lvl)).astype(BF16)
        scores = scores + _dot_nt(z, z) * m_ref[1 + lvl]
    o = o + _dot(scores.astype(BF16), vb)
    decay = jnp.broadcast_to(e_b[c - 1:c, :], (C_DK, C_DK)).T
    s_new = s * decay + _dot_tn((kk * e_e).astype(BF16), vb)
    return o, s_new


def _head_out(o, gx, gn):
    o = o * lax.rsqrt(jnp.mean(o * o, axis=-1, keepdims=True) + NORM_EPS) * gn
    return o * _silu(gx)


def _mixc_kernel(x_ref, g_ref, wq_ref, lb_ref, gn_ref, wout_ref, a_ref, m_ref,
                 o_ref, sout_ref, h_ref, p_ref, om_ref, s_ref, *, tt):
    j = pl.program_id(1)

    @pl.when(j == 0)
    def _():
        s_ref[...] = jnp.zeros_like(s_ref)

    x = x_ref[...]
    h_ref[...] = _rms(x, g_ref[...]).astype(BF16)
    nb = 4 * V7X_LANES
    for b in range(4 * D_MODEL // nb):
        pb = _dot(h_ref[...], wq_ref[:, b * nb:(b + 1) * nb])
        for k in range(nb // V7X_LANES):
            p_ref[b * (nb // V7X_LANES) + k] = pb[:, k * V7X_LANES:(k + 1) * V7X_LANES]

    def head_body(hd, carry):
        lb = lb_ref[hd]
        gn = gn_ref[...]

        def chunk_body(ci, s):
            rows = pl.ds(pl.multiple_of(ci * HG_CHUNK, HG_CHUNK), HG_CHUNK)
            o, s = _hgrn_chunk(p_ref[hd, rows, :], p_ref[C_HEADS + hd, rows, :],
                               p_ref[2 * C_HEADS + hd, rows, :], s, lb, a_ref, m_ref)
            om_ref[hd, rows, :] = _head_out(o, p_ref[3 * C_HEADS + hd, rows, :], gn).astype(BF16)
            return s

        s_ref[hd] = lax.fori_loop(0, tt // HG_CHUNK, chunk_body, s_ref[hd])
        return carry

    lax.fori_loop(0, C_HEADS, head_body, 0)
    om = jnp.concatenate([om_ref[hd] for hd in range(C_HEADS)], axis=1)
    o_ref[...] = x + _dot(om, wout_ref[...])

    @pl.when(j == pl.num_programs(1) - 1)
    def _():
        sout_ref[...] = s_ref[...]


def _mixc_prompt(x, g, wq, lb, gn, wout, a_stack, m_stack):
    n, t, _ = x.shape
    tt = TOK_TILE
    return pl.pallas_call(
        functools.partial(_mixc_kernel, tt=tt),
        out_shape=(jax.ShapeDtypeStruct((n, t, D_MODEL), F32),
                   jax.ShapeDtypeStruct((n, C_HEADS, C_DK, C_DV), F32)),
        grid=(n, t // tt),
        in_specs=[pl.BlockSpec((None, tt, D_MODEL), lambda i, j: (i, j, 0)),
                  _const_spec((1, D_MODEL)),
                  _const_spec((D_MODEL, 4 * D_MODEL)),
                  _const_spec((C_HEADS, 1, C_DK)),
                  _const_spec((1, C_DV)),
                  _const_spec((D_MODEL, D_MODEL)),
                  _const_spec(a_stack.shape),
                  _const_spec(m_stack.shape)],
        out_specs=(pl.BlockSpec((None, tt, D_MODEL), lambda i, j: (i, j, 0)),
                   pl.BlockSpec((None, C_HEADS, C_DK, C_DV), lambda i, j: (i, 0, 0, 0))),
        scratch_shapes=[pltpu.VMEM((tt, D_MODEL), BF16),
                        pltpu.VMEM((4 * C_HEADS, tt, V7X_LANES), F32),
                        pltpu.VMEM((C_HEADS, tt, V7X_LANES), BF16),
                        pltpu.VMEM((C_HEADS, C_DK, C_DV), F32)],
        compiler_params=_params("arbitrary", "arbitrary"),
        name="mix_hgrn2",
    )(x, g, wq, lb, gn, wout, a_stack, m_stack)


def _mixa_sample_kernel(x_ref, st_ref, g_ref, win_ref, cw_ref, wout_ref, o_ref, sto_ref):
    x = x_ref[...]
    h = _rms(x, g_ref[...]).astype(BF16)
    p = _dot(h, win_ref[...])
    bg = p[:, 0:D_MODEL]
    u = p[:, D_MODEL:2 * D_MODEL] * p[:, 2 * D_MODEL:3 * D_MODEL]
    y = cw_ref[0:1, :] * st_ref[0] + cw_ref[1:2, :] * st_ref[1] + cw_ref[2:3, :] * u
    o_ref[...] = x + _dot((bg * y).astype(BF16), wout_ref[...])
    sto_ref[0] = st_ref[1]
    sto_ref[1] = u


def _mixa_sample(x2d, st, g, win, cw, wout):
    n = x2d.shape[0]
    return pl.pallas_call(
        _mixa_sample_kernel,
        out_shape=(jax.ShapeDtypeStruct((n, D_MODEL), F32),
                   jax.ShapeDtypeStruct((A_WIDTH - 1, n, D_MODEL), F32)),
        grid=(1,),
        in_specs=[_const_spec((n, D_MODEL)),
                  _const_spec((A_WIDTH - 1, n, D_MODEL)),
                  _const_spec((1, D_MODEL)),
                  _const_spec((D_MODEL, 3 * D_MODEL)),
                  _const_spec((A_WIDTH, D_MODEL)),
                  _const_spec((D_MODEL, D_MODEL))],
        out_specs=(pl.BlockSpec((n, D_MODEL), lambda i: (0, 0)),
                   pl.BlockSpec((A_WIDTH - 1, n, D_MODEL), lambda i: (0, 0, 0))),
        compiler_params=_params("arbitrary"),
        name="mix_shortconv_sample",
    )(x2d, st, g, win, cw, wout)


def _mixb_sample_kernel(x_ref, st_ref, g_ref, w1_ref, b1_ref, dw_ref, dwb_ref, lng_ref, lnb_ref,
                        w2_ref, b2_ref, o_ref, sto_ref, u_ref, y_ref, *, nb):
    i = pl.program_id(0)
    hist = B_WIDTH - 1

    @pl.when(i == 0)
    def _():
        h = _rms(x_ref[...], g_ref[...]).astype(BF16)
        p = _dot(h, w1_ref[...]) + b1_ref[...]
        u_ref[...] = p[:, 0:D_MODEL] * jax.nn.sigmoid(p[:, D_MODEL:2 * D_MODEL])

    rows = pl.ds(pl.multiple_of(i * nb, nb), nb)
    u = u_ref[rows, :]
    st = st_ref[...]
    y_ref[rows, :] = jnp.sum(st * dw_ref[0:hist, :], axis=1) + dw_ref[hist:hist + 1, :] * u
    sto_ref[:, 0:hist - 1, :] = st[:, 1:hist, :]
    sto_ref[:, hist - 1, :] = u

    @pl.when(i == pl.num_programs(0) - 1)
    def _():
        z = _silu(_layer_norm(y_ref[...] + dwb_ref[...], lng_ref[...], lnb_ref[...]))
        o_ref[...] = x_ref[...] + _dot(z.astype(BF16), w2_ref[...]) + b2_ref[...]


def _mixb_sample(x2d, st, g, w1, b1, dw, dwb, lng, lnb, w2, b2):
    n = x2d.shape[0]
    nb = SAMPLE_BLOCK_B
    hist = B_WIDTH - 1
    return pl.pallas_call(
        functools.partial(_mixb_sample_kernel, nb=nb),
        out_shape=(jax.ShapeDtypeStruct((n, D_MODEL), F32),
                   jax.ShapeDtypeStruct((n, hist, D_MODEL), F32)),
        grid=(n // nb,),
        in_specs=[_const_spec((n, D_MODEL)),
                  pl.BlockSpec((nb, hist, D_MODEL), lambda i: (i, 0, 0)),
                  _const_spec((1, D_MODEL)),
                  _const_spec((D_MODEL, 2 * D_MODEL)),
                  _const_spec((1, 2 * D_MODEL)),
                  _const_spec((B_WIDTH, D_MODEL)),
                  _const_spec((1, D_MODEL)),
                  _const_spec((1, D_MODEL)),
                  _const_spec((1, D_MODEL)),
                  _const_spec((D_MODEL, D_MODEL)),
                  _const_spec((1, D_MODEL))],
        out_specs=(pl.BlockSpec((n, D_MODEL), lambda i: (0, 0)),
                   pl.BlockSpec((nb, hist, D_MODEL), lambda i: (i, 0, 0))),
        scratch_shapes=[pltpu.VMEM((n, D_MODEL), F32),
                        pltpu.VMEM((n, D_MODEL), F32)],
        compiler_params=_params("arbitrary"),
        name="mix_conformer_sample",
    )(x2d, st, g, w1, b1, dw, dwb, lng, lnb, w2, b2)


def _mixc_sample_kernel(x_ref, s_ref, g_ref, wq_ref, lb_ref, gn_ref, wout_ref,
                        o_ref, so_ref, p_ref, orow_ref, om_ref, *, nb):
    i = pl.program_id(0)

    @pl.when(i == 0)
    def _():
        h = _rms(x_ref[...], g_ref[...]).astype(BF16)
        p_ref[...] = _dot(h, wq_ref[...])

    rows = pl.ds(pl.multiple_of(i * nb, nb), nb)
    d = D_MODEL
    f, kk = _gates(p_ref[rows, d:2 * d], lb_ref[...])
    q = _silu(p_ref[rows, 0:d]) * (C_DK ** -0.5)
    v = p_ref[rows, 2 * d:3 * d]

    def column(a, r, lanes):
        return jnp.broadcast_to(a[r:r + 1, lanes], (C_DK, C_DK)).T

    for r in range(nb):
        for hd in range(C_HEADS):
            lanes = slice(hd * C_DK, (hd + 1) * C_DK)
            s_new = column(f, r, lanes) * s_ref[r, hd] + column(kk, r, lanes) * v[r:r + 1, lanes]
            so_ref[r, hd] = s_new
            orow_ref[r:r + 1, lanes] = jnp.sum(column(q, r, lanes) * s_new, axis=0, keepdims=True)

    gx = p_ref[rows, 3 * d:4 * d]
    for hd in range(C_HEADS):
        lanes = slice(hd * C_DK, (hd + 1) * C_DK)
        om_ref[rows, lanes] = _head_out(orow_ref[:, lanes], gx[:, lanes], gn_ref[...])

    @pl.when(i == pl.num_programs(0) - 1)
    def _():
        o_ref[...] = x_ref[...] + _dot(om_ref[...].astype(BF16), wout_ref[...])


def _mixc_sample(x2d, s, g, wq, lb_row, gn, wout):
    n = x2d.shape[0]
    nb = SAMPLE_BLOCK_C
    return pl.pallas_call(
        functools.partial(_mixc_sample_kernel, nb=nb),
        out_shape=(jax.ShapeDtypeStruct((n, D_MODEL), F32),
                   jax.ShapeDtypeStruct((n, C_HEADS, C_DK, C_DV), F32)),
        grid=(n // nb,),
        in_specs=[_const_spec((n, D_MODEL)),
                  pl.BlockSpec((nb, C_HEADS, C_DK, C_DV), lambda i: (i, 0, 0, 0)),
                  _const_spec((1, D_MODEL)),
                  _const_spec((D_MODEL, 4 * D_MODEL)),
                  _const_spec((1, D_MODEL)),
                  _const_spec((1, C_DV)),
                  _const_spec((D_MODEL, D_MODEL))],
        out_specs=(pl.BlockSpec((n, D_MODEL), lambda i: (0, 0)),
                   pl.BlockSpec((nb, C_HEADS, C_DK, C_DV), lambda i: (i, 0, 0, 0))),
        scratch_shapes=[pltpu.VMEM((n, 4 * D_MODEL), F32),
                        pltpu.VMEM((nb, D_MODEL), F32),
                        pltpu.VMEM((n, D_MODEL), F32)],
        compiler_params=_params("arbitrary"),
        name="mix_hgrn2_sample",
    )(x2d, s, g, wq, lb_row, gn, wout)


def _row(a):
    return a.reshape(1, -1)


def kernel(x_prompt, x_sample, state_conva, state_convb, state_hgrn, norm_mix, a_w_in, a_conv_w, a_w_out, b_w_pw1, b_b_pw1, b_dw_w, b_dw_b, b_ln_g, b_ln_b, b_w_pw2, b_b_pw2, c_lower_bounds, c_w_qfig, c_gnorm, c_w_out, norm_ffn, ffn_w_gate_up, ffn_w_down, norm_final):
    nb, t, d = x_prompt.shape
    ns = x_sample.shape[0]
    bf = lambda w: w.astype(BF16)
    a_w_in, a_w_out, b_w_pw1, b_w_pw2 = bf(a_w_in), bf(a_w_out), bf(b_w_pw1), bf(b_w_pw2)
    c_w_qfig, c_w_out, w_gu, w_dn = bf(c_w_qfig), bf(c_w_out), bf(ffn_w_gate_up), bf(ffn_w_down)

    sm = jax.nn.softmax(c_lower_bounds.astype(F32), axis=0)
    lower = jnp.cumsum(sm, axis=0) - sm[0]
    a_stack, m_stack = _hgrn_constants()
    gfin = _row(norm_final)

    xp = x_prompt
    xs = x_sample.reshape(ns, d)
    conva_p, conva_s, convb_p, convb_s, hgrn_p, hgrn_s = [], [], [], [], [], []
    for i in range(DEPTH):
        kind, j = i % 3, i // 3
        g = _row(norm_mix[i])
        if kind == 0:
            w = (g, a_w_in[j], a_conv_w[j], a_w_out[j])
            xp, st = _mixa_prompt(xp, *w)
            conva_p.append(st)
            xs, st = _mixa_sample(xs, jnp.swapaxes(state_conva[j], 0, 1), *w)
            conva_s.append(jnp.swapaxes(st, 0, 1))
        elif kind == 1:
            w = (g, b_w_pw1[j], _row(b_b_pw1[j]), b_dw_w[j], _row(b_dw_b[j]), _row(b_ln_g[j]),
                 _row(b_ln_b[j]), b_w_pw2[j], _row(b_b_pw2[j]))
            xp, st = _mixb_prompt(xp, *w)
            convb_p.append(st)
            xs, st = _mixb_sample(xs, state_convb[j], *w)
            convb_s.append(st)
        else:
            gn = _row(c_gnorm[j])
            xp, st = _mixc_prompt(xp, g, c_w_qfig[j], lower[i].reshape(C_HEADS, 1, C_DK), gn,
                                  c_w_out[j], a_stack, m_stack)
            hgrn_p.append(st)
            xs, st = _mixc_sample(xs, state_hgrn[j], g, c_w_qfig[j], _row(lower[i]), gn, c_w_out[j])
            hgrn_s.append(st)
        final = i == DEPTH - 1
        gf = _row(norm_ffn[i])
        xp = _ffn(xp.reshape(nb * t, d), gf, w_gu[i], w_dn[i], gfin, tm=TOK_TILE,
                  final=final).reshape(nb, t, d)
        xs = _ffn(xs, gf, w_gu[i], w_dn[i], gfin, tm=ns, final=final)
    stack = lambda xs_: xs_[0][None] if len(xs_) == 1 else jnp.stack(xs_)
    return (xp, xs.reshape(ns, 1, d), stack(conva_p), stack(conva_s), stack(convb_p),
            stack(convb_s), stack(hgrn_p), stack(hgrn_s))
```

```python
import functools

import numpy as np
import jax
import jax.numpy as jnp
from jax import lax
from jax.experimental import pallas as pl
from jax.experimental.pallas import tpu as pltpu

F32 = jnp.float32
BF16 = jnp.bfloat16

D_MODEL = 1024
DEPTH = 4
D_FF = 2816
A_WIDTH = 3
B_WIDTH = 31
C_HEADS = 8
C_DK = 128
C_DV = 128
NORM_EPS = 1e-6
LN_EPS = 1e-5

V7X_SUBLANES = 8
V7X_LANES = 128
V7X_VMEM_LIMIT_BYTES = 56 * 1024 * 1024

TOK_TILE = 512
FF_CHUNK = 256
HG_CHUNK = 128
HG_LEVELS = 7
SAMPLE_BLOCK_B = 16
SAMPLE_BLOCK_C = 8
A_HALO = V7X_SUBLANES
B_HALO = 32


def _params(*sem):
    return pltpu.CompilerParams(dimension_semantics=sem,
                                vmem_limit_bytes=V7X_VMEM_LIMIT_BYTES)


def _const_spec(shape):
    nd = len(shape)
    return pl.BlockSpec(shape, lambda *_: (0,) * nd, pipeline_mode=pl.Buffered(1))


def _rms(x, g):
    ms = jnp.mean(x * x, axis=-1, keepdims=True)
    return x * lax.rsqrt(ms + NORM_EPS) * g


def _silu(x):
    return x * jax.nn.sigmoid(x)


def _dot(a, b):
    return jnp.dot(a, b, preferred_element_type=F32)


def _dot_nt(a, b):
    return lax.dot_general(a, b, (((1,), (1,)), ((), ())), preferred_element_type=F32)


def _dot_tn(a, b):
    return lax.dot_general(a, b, (((0,), (0,)), ((), ())), preferred_element_type=F32)


def _ffn_kernel(x_ref, g_ref, wgu_ref, wd_ref, gf_ref, o_ref, h_ref, a_ref, *, final):
    x = x_ref[...]
    h_ref[...] = _rms(x, g_ref[...]).astype(BF16)
    for j in range(D_FF // FF_CHUNK):
        lo = j * FF_CHUNK
        gate = _dot(h_ref[...], wgu_ref[:, lo:lo + FF_CHUNK])
        up = _dot(h_ref[...], wgu_ref[:, D_FF + lo:D_FF + lo + FF_CHUNK])
        a_ref[:, lo:lo + FF_CHUNK] = (_silu(gate) * up).astype(BF16)
    y = x + _dot(a_ref[...], wd_ref[...])
    if final:
        y = _rms(y, gf_ref[...])
    o_ref[...] = y


def _ffn(x2d, g, wgu, wd, gf, *, tm, final):
    m = x2d.shape[0]
    return pl.pallas_call(
        functools.partial(_ffn_kernel, final=final),
        out_shape=jax.ShapeDtypeStruct((m, D_MODEL), F32),
        grid=(m // tm,),
        in_specs=[pl.BlockSpec((tm, D_MODEL), lambda i: (i, 0)),
                  _const_spec((1, D_MODEL)),
                  _const_spec((D_MODEL, 2 * D_FF)),
                  _const_spec((D_FF, D_MODEL)),
                  _const_spec((1, D_MODEL))],
        out_specs=pl.BlockSpec((tm, D_MODEL), lambda i: (i, 0)),
        scratch_shapes=[pltpu.VMEM((tm, D_MODEL), BF16),
                        pltpu.VMEM((tm, D_FF), BF16)],
        compiler_params=_params("arbitrary"),
        name="ffn",
    )(x2d, g, wgu, wd, gf)


def _carry_halo(ubuf_ref, halo, tt):
    j = pl.program_id(1)

    @pl.when(j == 0)
    def _():
        ubuf_ref[0:halo, :] = jnp.zeros((halo, D_MODEL), F32)

    @pl.when(j > 0)
    def _():
        ubuf_ref[0:halo, :] = ubuf_ref[tt:tt + halo, :]


def _mixa_kernel(x_ref, g_ref, win_ref, cw_ref, wout_ref, o_ref, st_ref, h_ref, ubuf_ref, *, tt):
    x = x_ref[...]
    h_ref[...] = _rms(x, g_ref[...]).astype(BF16)
    _carry_halo(ubuf_ref, A_HALO, tt)
    cg = _dot(h_ref[...], win_ref[:, D_MODEL:2 * D_MODEL])
    hv = _dot(h_ref[...], win_ref[:, 2 * D_MODEL:3 * D_MODEL])
    ubuf_ref[A_HALO:A_HALO + tt, :] = cg * hv
    y = cw_ref[A_WIDTH - 1:A_WIDTH, :] * ubuf_ref[A_HALO:A_HALO + tt, :]
    for k in range(A_WIDTH - 1):
        off = A_HALO - (A_WIDTH - 1) + k
        y = y + cw_ref[k:k + 1, :] * ubuf_ref[off:off + tt, :]
    bg = _dot(h_ref[...], win_ref[:, 0:D_MODEL])
    o_ref[...] = x + _dot((bg * y).astype(BF16), wout_ref[...])

    @pl.when(pl.program_id(1) == pl.num_programs(1) - 1)
    def _():
        st_ref[...] = ubuf_ref[A_HALO + tt - (A_WIDTH - 1):A_HALO + tt, :]


def _mixa_prompt(x, g, win, cw, wout):
    n, t, _ = x.shape
    tt = TOK_TILE
    return pl.pallas_call(
        functools.partial(_mixa_kernel, tt=tt),
        out_shape=(jax.ShapeDtypeStruct((n, t, D_MODEL), F32),
                   jax.ShapeDtypeStruct((n, A_WIDTH - 1, D_MODEL), F32)),
        grid=(n, t // tt),
        in_specs=[pl.BlockSpec((None, tt, D_MODEL), lambda i, j: (i, j, 0)),
                  _const_spec((1, D_MODEL)),
                  _const_spec((D_MODEL, 3 * D_MODEL)),
                  _const_spec((A_WIDTH, D_MODEL)),
                  _const_spec((D_MODEL, D_MODEL))],
        out_specs=(pl.BlockSpec((None, tt, D_MODEL), lambda i, j: (i, j, 0)),
                   pl.BlockSpec((None, A_WIDTH - 1, D_MODEL), lambda i, j: (i, 0, 0))),
        scratch_shapes=[pltpu.VMEM((tt, D_MODEL), BF16),
                        pltpu.VMEM((A_HALO + tt, D_MODEL), F32)],
        compiler_params=_params("arbitrary", "arbitrary"),
        name="mix_shortconv",
    )(x, g, win, cw, wout)


def _layer_norm(y, g, b):
    mu = jnp.mean(y, axis=-1, keepdims=True)
    yc = y - mu
    var = jnp.mean(yc * yc, axis=-1, keepdims=True)
    return yc * lax.rsqrt(var + LN_EPS) * g + b


def _mixb_kernel(x_ref, g_ref, w1_ref, b1_ref, dw_ref, dwb_ref, lng_ref, lnb_ref, w2_ref, b2_ref,
                 o_ref, st_ref, h_ref, ubuf_ref, *, tt):
    x = x_ref[...]
    h_ref[...] = _rms(x, g_ref[...]).astype(BF16)
    _carry_halo(ubuf_ref, B_HALO, tt)
    a = _dot(h_ref[...], w1_ref[:, 0:D_MODEL]) + b1_ref[:, 0:D_MODEL]
    gate = _dot(h_ref[...], w1_ref[:, D_MODEL:2 * D_MODEL]) + b1_ref[:, D_MODEL:2 * D_MODEL]
    ubuf_ref[B_HALO:B_HALO + tt, :] = a * jax.nn.sigmoid(gate)
    base = B_HALO - (B_WIDTH - 1)
    y = dw_ref[0:1, :] * ubuf_ref[base:base + tt, :]
    for k in range(1, B_WIDTH):
        y = y + dw_ref[k:k + 1, :] * ubuf_ref[base + k:base + k + tt, :]
    z = _silu(_layer_norm(y + dwb_ref[...], lng_ref[...], lnb_ref[...]))
    o_ref[...] = x + _dot(z.astype(BF16), w2_ref[...]) + b2_ref[...]

    @pl.when(pl.program_id(1) == pl.num_programs(1) - 1)
    def _():
        st_ref[...] = ubuf_ref[B_HALO + tt - (B_WIDTH - 1):B_HALO + tt, :]


def _mixb_prompt(x, g, w1, b1, dw, dwb, lng, lnb, w2, b2):
    n, t, _ = x.shape
    tt = TOK_TILE
    return pl.pallas_call(
        functools.partial(_mixb_kernel, tt=tt),
        out_shape=(jax.ShapeDtypeStruct((n, t, D_MODEL), F32),
                   jax.ShapeDtypeStruct((n, B_WIDTH - 1, D_MODEL), F32)),
        grid=(n, t // tt),
        in_specs=[pl.BlockSpec((None, tt, D_MODEL), lambda i, j: (i, j, 0)),
                  _const_spec((1, D_MODEL)),
                  _const_spec((D_MODEL, 2 * D_MODEL)),
                  _const_spec((1, 2 * D_MODEL)),
                  _const_spec((B_WIDTH, D_MODEL)),
                  _const_spec((1, D_MODEL)),
                  _const_spec((1, D_MODEL)),
                  _const_spec((1, D_MODEL)),
                  _const_spec((D_MODEL, D_MODEL)),
                  _const_spec((1, D_MODEL))],
        out_specs=(pl.BlockSpec((None, tt, D_MODEL), lambda i, j: (i, j, 0)),
                   pl.BlockSpec((None, B_WIDTH - 1, D_MODEL), lambda i, j: (i, 0, 0))),
        scratch_shapes=[pltpu.VMEM((tt, D_MODEL), BF16),
                        pltpu.VMEM((B_HALO + tt, D_MODEL), F32)],
        compiler_params=_params("arbitrary", "arbitrary"),
        name="mix_conformer",
    )(x, g, w1, b1, dw, dwb, lng, lnb, w2, b2)


def _hgrn_constants():
    c = HG_CHUNK
    t = np.arange(c)[:, None]
    r = np.arange(c)[None, :]
    blocks = [(r <= t), (r > t)]
    masks = [(t == r)]
    for lvl in range(HG_LEVELS):
        h = 1 << lvl
        m = (t // (2 * h)) * (2 * h) + h - 1
        upper = (t & h) != 0
        blocks.append(np.where(upper, (r > m) & (r <= t), (r > t) & (r <= m)))
        masks.append((((t ^ r) >> lvl) == 1) & upper)
    a_stack = np.concatenate(blocks, axis=0).astype(np.float32)
    a_stack = np.concatenate([a_stack, a_stack], axis=1)
    m_stack = np.stack(masks, axis=0).astype(np.float32)
    return jnp.asarray(a_stack, BF16), jnp.asarray(m_stack, F32)


def _gates(fx, lb):
    sig = jax.nn.sigmoid(fx)
    return lb + (1.0 - lb) * sig, (1.0 - lb) * (1.0 - sig)


def _head_out(o, gx, gn):
    o = o * lax.rsqrt(jnp.mean(o * o, axis=-1, keepdims=True) + NORM_EPS) * gn
    return o * _silu(gx)


def _hgrn_head_chunk(q, kk, v, s, e, m_ref):
    c = HG_CHUNK
    e_b = e(0)
    vb = v.astype(BF16)
    o = _dot((q * e_b).astype(BF16), s.astype(BF16))
    scores = _dot_nt(q.astype(BF16), kk.astype(BF16)) * m_ref[0]
    for lvl in range(HG_LEVELS):
        e_l = e(2 + lvl)
        scores = scores + _dot_nt((q * e_l).astype(BF16), (kk * e_l).astype(BF16)) * m_ref[1 + lvl]
    o = o + _dot(scores.astype(BF16), vb)
    decay = jnp.broadcast_to(e_b[c - 1:c, :], (C_DK, C_DK)).T
    s_new = s * decay + _dot_tn((kk * e(1)).astype(BF16), vb)
    return o, s_new


def _mixc_kernel(x_ref, g_ref, wq_ref, lb_ref, gn_ref, wout_ref, a_ref, m_ref,
                 o_ref, sout_ref, h_ref, p_ref, e_ref, om_ref, s_ref, *, tt):
    j = pl.program_id(1)
    d = D_MODEL
    c = HG_CHUNK

    @pl.when(j == 0)
    def _():
        s_ref[...] = jnp.zeros_like(s_ref)

    x = x_ref[...]
    h_ref[...] = _rms(x, g_ref[...]).astype(BF16)
    nb = 4 * V7X_LANES
    for b in range(4 * d // nb):
        p_ref[:, b * nb:(b + 1) * nb] = _dot(h_ref[...], wq_ref[:, b * nb:(b + 1) * nb])

    def chunk_body(ci, carry):
        rows = pl.ds(pl.multiple_of(ci * c, c), c)
        f, kk = _gates(p_ref[rows, d:2 * d], lb_ref[...])
        logf = jnp.log(f)
        hi = logf.astype(BF16)
        lo = (logf - hi.astype(F32)).astype(BF16)
        e_ref[...] = jnp.exp(_dot(a_ref[...], jnp.concatenate([hi, lo], axis=0)))
        p_ref[rows, d:2 * d] = kk
        p_ref[rows, 0:d] = _silu(p_ref[rows, 0:d]) * (C_DK ** -0.5)
        for hd in range(C_HEADS):
            lanes = slice(hd * C_DK, (hd + 1) * C_DK)
            o, s_ref[hd] = _hgrn_head_chunk(
                p_ref[rows, lanes], p_ref[rows, d + hd * C_DK:d + (hd + 1) * C_DK],
                p_ref[rows, 2 * d + hd * C_DV:2 * d + (hd + 1) * C_DV], s_ref[hd],
                lambda r: e_ref[r * c:(r + 1) * c, lanes], m_ref)
            gx = p_ref[rows, 3 * d + hd * C_DV:3 * d + (hd + 1) * C_DV]
            om_ref[rows, lanes] = _head_out(o, gx, gn_ref[...]).astype(BF16)
        return carry

    lax.fori_loop(0, tt // c, chunk_body, 0)
    o_ref[...] = x + _dot(om_ref[...], wout_ref[...])

    @pl.when(j == pl.num_programs(1) - 1)
    def _():
        sout_ref[...] = s_ref[...]


def _mixc_prompt(x, g, wq, lb_row, gn, wout, a_stack, m_stack):
    n, t, _ = x.shape
    tt = TOK_TILE
    return pl.pallas_call(
        functools.partial(_mixc_kernel, tt=tt),
        out_shape=(jax.ShapeDtypeStruct((n, t, D_MODEL), F32),
                   jax.ShapeDtypeStruct((n, C_HEADS, C_DK, C_DV), F32)),
        grid=(n, t // tt),
        in_specs=[pl.BlockSpec((None, tt, D_MODEL), lambda i, j: (i, j, 0)),
                  _const_spec((1, D_MODEL)),
                  _const_spec((D_MODEL, 4 * D_MODEL)),
                  _const_spec((1, D_MODEL)),
                  _const_spec((1, C_DV)),
                  _const_spec((D_MODEL, D_MODEL)),
                  _const_spec(a_stack.shape),
                  _const_spec(m_stack.shape)],
        out_specs=(pl.BlockSpec((None, tt, D_MODEL), lambda i, j: (i, j, 0)),
                   pl.BlockSpec((None, C_HEADS, C_DK, C_DV), lambda i, j: (i, 0, 0, 0))),
        scratch_shapes=[pltpu.VMEM((tt, D_MODEL), BF16),
                        pltpu.VMEM((tt, 4 * D_MODEL), F32),
                        pltpu.VMEM(((2 + HG_LEVELS) * HG_CHUNK, D_MODEL), F32),
                        pltpu.VMEM((tt, D_MODEL), BF16),
                        pltpu.VMEM((C_HEADS, C_DK, C_DV), F32)],
        compiler_params=_params("arbitrary", "arbitrary"),
        name="mix_hgrn2",
    )(x, g, wq, lb_row, gn, wout, a_stack, m_stack)


def _mixa_sample_kernel(x_ref, st_ref, g_ref, win_ref, cw_ref, wout_ref, o_ref, sto_ref):
    x = x_ref[...]
    h = _rms(x, g_ref[...]).astype(BF16)
    p = _dot(h, win_ref[...])
    bg = p[:, 0:D_MODEL]
    u = p[:, D_MODEL:2 * D_MODEL] * p[:, 2 * D_MODEL:3 * D_MODEL]
    y = cw_ref[0:1, :] * st_ref[0] + cw_ref[1:2, :] * st_ref[1] + cw_ref[2:3, :] * u
    o_ref[...] = x + _dot((bg * y).astype(BF16), wout_ref[...])
    sto_ref[0] = st_ref[1]
    sto_ref[1] = u


def _mixa_sample(x2d, st, g, win, cw, wout):
    n = x2d.shape[0]
    return pl.pallas_call(
        _mixa_sample_kernel,
        out_shape=(jax.ShapeDtypeStruct((n, D_MODEL), F32),
                   jax.ShapeDtypeStruct((A_WIDTH - 1, n, D_MODEL), F32)),
        grid=(1,),
        in_specs=[_const_spec((n, D_MODEL)),
                  _const_spec((A_WIDTH - 1, n, D_MODEL)),
                  _const_spec((1, D_MODEL)),
                  _const_spec((D_MODEL, 3 * D_MODEL)),
                  _const_spec((A_WIDTH, D_MODEL)),
                  _const_spec((D_MODEL, D_MODEL))],
        out_specs=(pl.BlockSpec((n, D_MODEL), lambda i: (0, 0)),
                   pl.BlockSpec((A_WIDTH - 1, n, D_MODEL), lambda i: (0, 0, 0))),
        compiler_params=_params("arbitrary"),
        name="mix_shortconv_sample",
    )(x2d, st, g, win, cw, wout)


def _mixb_sample_kernel(x_ref, st_ref, g_ref, w1_ref, b1_ref, dw_ref, dwb_ref, lng_ref, lnb_ref,
                        w2_ref, b2_ref, o_ref, sto_ref, u_ref, y_ref, *, nb):
    i = pl.program_id(0)
    hist = B_WIDTH - 1

    @pl.when(i == 0)
    def _():
        h = _rms(x_ref[...], g_ref[...]).astype(BF16)
        p = _dot(h, w1_ref[...]) + b1_ref[...]
        u_ref[...] = p[:, 0:D_MODEL] * jax.nn.sigmoid(p[:, D_MODEL:2 * D_MODEL])

    rows = pl.ds(pl.multiple_of(i * nb, nb), nb)
    u = u_ref[rows, :]
    st = st_ref[...]
    y_ref[rows, :] = jnp.sum(st * dw_ref[0:hist, :], axis=1) + dw_ref[hist:hist + 1, :] * u
    sto_ref[:, 0:hist - 1, :] = st[:, 1:hist, :]
    sto_ref[:, hist - 1, :] = u

    @pl.when(i == pl.num_programs(0) - 1)
    def _():
        z = _silu(_layer_norm(y_ref[...] + dwb_ref[...], lng_ref[...], lnb_ref[...]))
        o_ref[...] = x_ref[...] + _dot(z.astype(BF16), w2_ref[...]) + b2_ref[...]


def _mixb_sample(x2d, st, g, w1, b1, dw, dwb, lng, lnb, w2, b2):
    n = x2d.shape[0]
    nb = SAMPLE_BLOCK_B
    hist = B_WIDTH - 1
    return pl.pallas_call(
        functools.partial(_mixb_sample_kernel, nb=nb),
        out_shape=(jax.ShapeDtypeStruct((n, D_MODEL), F32),
                   jax.ShapeDtypeStruct((n, hist, D_MODEL), F32)),
        grid=(n // nb,),
        in_specs=[_const_spec((n, D_MODEL)),
                  pl.BlockSpec((nb, hist, D_MODEL), lambda i: (i, 0, 0)),
                  _const_spec((1, D_MODEL)),
                  _const_spec((D_MODEL, 2 * D_MODEL)),
                  _const_spec((1, 2 * D_MODEL)),
                  _const_spec((B_WIDTH, D_MODEL)),
                  _const_spec((1, D_MODEL)),
                  _const_spec((1, D_MODEL)),
                  _const_spec((1, D_MODEL)),
                  _const_spec((D_MODEL, D_MODEL)),
                  _const_spec((1, D_MODEL))],
        out_specs=(pl.BlockSpec((n, D_MODEL), lambda i: (0, 0)),
                   pl.BlockSpec((nb, hist, D_MODEL), lambda i: (i, 0, 0))),
        scratch_shapes=[pltpu.VMEM((n, D_MODEL), F32),
                        pltpu.VMEM((n, D_MODEL), F32)],
        compiler_params=_params("arbitrary"),
        name="mix_conformer_sample",
    )(x2d, st, g, w1, b1, dw, dwb, lng, lnb, w2, b2)


def _mixc_sample_kernel(x_ref, s_ref, g_ref, wq_ref, lb_ref, gn_ref, wout_ref,
                        o_ref, so_ref, p_ref, orow_ref, om_ref, *, nb):
    i = pl.program_id(0)

    @pl.when(i == 0)
    def _():
        h = _rms(x_ref[...], g_ref[...]).astype(BF16)
        p_ref[...] = _dot(h, wq_ref[...])

    rows = pl.ds(pl.multiple_of(i * nb, nb), nb)
    d = D_MODEL
    f, kk = _gates(p_ref[rows, d:2 * d], lb_ref[...])
    q = _silu(p_ref[rows, 0:d]) * (C_DK ** -0.5)
    v = p_ref[rows, 2 * d:3 * d]

    def column(a, r, lanes):
        return jnp.broadcast_to(a[r:r + 1, lanes], (C_DK, C_DK)).T

    for r in range(nb):
        for hd in range(C_HEADS):
            lanes = slice(hd * C_DK, (hd + 1) * C_DK)
            s_new = column(f, r, lanes) * s_ref[r, hd] + column(kk, r, lanes) * v[r:r + 1, lanes]
            so_ref[r, hd] = s_new
            orow_ref[r:r + 1, lanes] = jnp.sum(column(q, r, lanes) * s_new, axis=0, keepdims=True)

    gx = p_ref[rows, 3 * d:4 * d]
    for hd in range(C_HEADS):
        lanes = slice(hd * C_DK, (hd + 1) * C_DK)
        om_ref[rows, lanes] = _head_out(orow_ref[:, lanes], gx[:, lanes], gn_ref[...])

    @pl.when(i == pl.num_programs(0) - 1)
    def _():
        o_ref[...] = x_ref[...] + _dot(om_ref[...].astype(BF16), wout_ref[...])


def _mixc_sample(x2d, s, g, wq, lb_row, gn, wout):
    n = x2d.shape[0]
    nb = SAMPLE_BLOCK_C
    return pl.pallas_call(
        functools.partial(_mixc_sample_kernel, nb=nb),
        out_shape=(jax.ShapeDtypeStruct((n, D_MODEL), F32),
                   jax.ShapeDtypeStruct((n, C_HEADS, C_DK, C_DV), F32)),
        grid=(n // nb,),
        in_specs=[_const_spec((n, D_MODEL)),
                  pl.BlockSpec((nb, C_HEADS, C_DK, C_DV), lambda i: (i, 0, 0, 0)),
                  _const_spec((1, D_MODEL)),
                  _const_spec((D_MODEL, 4 * D_MODEL)),
                  _const_spec((1, D_MODEL)),
                  _const_spec((1, C_DV)),
                  _const_spec((D_MODEL, D_MODEL))],
        out_specs=(pl.BlockSpec((n, D_MODEL), lambda i: (0, 0)),
                   pl.BlockSpec((nb, C_HEADS, C_DK, C_DV), lambda i: (i, 0, 0, 0))),
        scratch_shapes=[pltpu.VMEM((n, 4 * D_MODEL), F32),
                        pltpu.VMEM((nb, D_MODEL), F32),
                        pltpu.VMEM((n, D_MODEL), F32)],
        compiler_params=_params("arbitrary"),
        name="mix_hgrn2_sample",
    )(x2d, s, g, wq, lb_row, gn, wout)


def _row(a):
    return a.reshape(1, -1)


def kernel(x_prompt, x_sample, state_conva, state_convb, state_hgrn, norm_mix, a_w_in, a_conv_w, a_w_out, b_w_pw1, b_b_pw1, b_dw_w, b_dw_b, b_ln_g, b_ln_b, b_w_pw2, b_b_pw2, c_lower_bounds, c_w_qfig, c_gnorm, c_w_out, norm_ffn, ffn_w_gate_up, ffn_w_down, norm_final):
    nb, t, d = x_prompt.shape
    ns = x_sample.shape[0]
    bf = lambda w: w.astype(BF16)
    a_w_in, a_w_out, b_w_pw1, b_w_pw2 = bf(a_w_in), bf(a_w_out), bf(b_w_pw1), bf(b_w_pw2)
    c_w_qfig, c_w_out, w_gu, w_dn = bf(c_w_qfig), bf(c_w_out), bf(ffn_w_gate_up), bf(ffn_w_down)

    sm = jax.nn.softmax(c_lower_bounds.astype(F32), axis=0)
    lower = jnp.cumsum(sm, axis=0) - sm[0]
    a_stack, m_stack = _hgrn_constants()
    gfin = _row(norm_final)

    xp = x_prompt
    xs = x_sample.reshape(ns, d)
    conva_p, conva_s, convb_p, convb_s, hgrn_p, hgrn_s = [], [], [], [], [], []
    for i in range(DEPTH):
        kind, j = i % 3, i // 3
        g = _row(norm_mix[i])
        if kind == 0:
            w = (g, a_w_in[j], a_conv_w[j], a_w_out[j])
            xp, st = _mixa_prompt(xp, *w)
            conva_p.append(st)
            xs, st = _mixa_sample(xs, jnp.swapaxes(state_conva[j], 0, 1), *w)
            conva_s.append(jnp.swapaxes(st, 0, 1))
        elif kind == 1:
            w = (g, b_w_pw1[j], _row(b_b_pw1[j]), b_dw_w[j], _row(b_dw_b[j]), _row(b_ln_g[j]),
                 _row(b_ln_b[j]), b_w_pw2[j], _row(b_b_pw2[j]))
            xp, st = _mixb_prompt(xp, *w)
            convb_p.append(st)
            xs, st = _mixb_sample(xs, state_convb[j], *w)
            convb_s.append(st)
        else:
            gn = _row(c_gnorm[j])
            xp, st = _mixc_prompt(xp, g, c_w_qfig[j], _row(lower[i]), gn, c_w_out[j], a_stack, m_stack)
            hgrn_p.append(st)
            xs, st = _mixc_sample(xs, state_hgrn[j], g, c_w_qfig[j], _row(lower[i]), gn, c_w_out[j])
            hgrn_s.append(st)
        final = i == DEPTH - 1
        gf = _row(norm_ffn[i])
        xp = _ffn(xp.reshape(nb * t, d), gf, w_gu[i], w_dn[i], gfin, tm=TOK_TILE,
                  final=final).reshape(nb, t, d)
        xs = _ffn(xs, gf, w_gu[i], w_dn[i], gfin, tm=ns, final=final)
    stack = lambda xs_: xs_[0][None] if len(xs_) == 1 else jnp.stack(xs_)
    return (xp, xs.reshape(ns, 1, d), stack(conva_p), stack(conva_s), stack(convb_p),
            stack(convb_s), stack(hgrn_p), stack(hgrn_s))
```

```python
import functools

import numpy as np
import jax
import jax.numpy as jnp
from jax import lax
from jax.experimental import pallas as pl
from jax.experimental.pallas import tpu as pltpu

F32 = jnp.float32
BF16 = jnp.bfloat16

D_MODEL = 1024
DEPTH = 4
D_FF = 2816
A_WIDTH = 3
B_WIDTH = 31
C_HEADS = 8
C_DK = 128
C_DV = 128
NORM_EPS = 1e-6
LN_EPS = 1e-5

V7X_SUBLANES = 8
V7X_LANES = 128
V7X_VMEM_LIMIT_BYTES = 56 * 1024 * 1024

TOK_TILE = 512
FF_CHUNK = 256
HG_CHUNK = 128
HG_LEVELS = 7
SAMPLE_BLOCK_B = 16
SAMPLE_BLOCK_C = 8
A_HALO = V7X_SUBLANES
B_HALO = 32
CONV_ROWS = 64


def _params(*sem):
    return pltpu.CompilerParams(dimension_semantics=sem,
                                vmem_limit_bytes=V7X_VMEM_LIMIT_BYTES)


def _const_spec(shape):
    nd = len(shape)
    return pl.BlockSpec(shape, lambda *_: (0,) * nd, pipeline_mode=pl.Buffered(1))


def _layer_spec(shape, layer):
    nd = len(shape)
    return pl.BlockSpec((None,) + tuple(shape), lambda *_: (layer,) + (0,) * nd,
                        pipeline_mode=pl.Buffered(1))


def _rms(x, g):
    ms = jnp.mean(x * x, axis=-1, keepdims=True)
    return x * lax.rsqrt(ms + NORM_EPS) * g


def _silu(x):
    return x * jax.nn.sigmoid(x)


def _dot(a, b):
    return jnp.dot(a, b, preferred_element_type=F32)


def _dot_nt(a, b):
    return lax.dot_general(a, b, (((1,), (1,)), ((), ())), preferred_element_type=F32)


def _dot_tn(a, b):
    return lax.dot_general(a, b, (((0,), (0,)), ((), ())), preferred_element_type=F32)


def _ffn_rows(x, g_ref, wgu_ref, wd_ref, gf_ref, h_ref, a_ref, final):
    h_ref[...] = _rms(x, g_ref[...]).astype(BF16)
    for j in range(D_FF // FF_CHUNK):
        lo = j * FF_CHUNK
        gate = _dot(h_ref[...], wgu_ref[:, lo:lo + FF_CHUNK])
        up = _dot(h_ref[...], wgu_ref[:, D_FF + lo:D_FF + lo + FF_CHUNK])
        a_ref[:, lo:lo + FF_CHUNK] = (_silu(gate) * up).astype(BF16)
    y = x + _dot(a_ref[...], wd_ref[...])
    return _rms(y, gf_ref[...]) if final else y


def _ffn_kernel(x_ref, xs_ref, g_ref, wgu_ref, wd_ref, gf_ref, o_ref, os_ref, h_ref, a_ref, *, final):
    args = (g_ref, wgu_ref, wd_ref, gf_ref)
    o_ref[...] = _ffn_rows(x_ref[...], *args, h_ref, a_ref, final)

    @pl.when(pl.program_id(0) == pl.num_programs(0) - 1)
    def _():
        ns = xs_ref.shape[0]
        os_ref[...] = _ffn_rows(xs_ref[...], *args, h_ref.at[0:ns], a_ref.at[0:ns], final)


def _ffn(x2d, xs, g3, wgu, wd, gf, *, layer, final):
    m = x2d.shape[0]
    ns = xs.shape[0]
    tm = TOK_TILE
    return pl.pallas_call(
        functools.partial(_ffn_kernel, final=final),
        out_shape=(jax.ShapeDtypeStruct((m, D_MODEL), F32),
                   jax.ShapeDtypeStruct((ns, D_MODEL), F32)),
        grid=(m // tm,),
        in_specs=[pl.BlockSpec((tm, D_MODEL), lambda i: (i, 0)),
                  _const_spec((ns, D_MODEL)),
                  _layer_spec((1, D_MODEL), layer),
                  _layer_spec((D_MODEL, 2 * D_FF), layer),
                  _layer_spec((D_FF, D_MODEL), layer),
                  _const_spec((1, D_MODEL))],
        out_specs=(pl.BlockSpec((tm, D_MODEL), lambda i: (i, 0)),
                   pl.BlockSpec((ns, D_MODEL), lambda i: (0, 0))),
        scratch_shapes=[pltpu.VMEM((tm, D_MODEL), BF16),
                        pltpu.VMEM((tm, D_FF), BF16)],
        compiler_params=_params("arbitrary"),
        name="ffn",
    )(x2d, xs, g3, wgu, wd, gf)


def _carry_halo(ubuf_ref, halo, tt):
    j = pl.program_id(1)

    @pl.when(j == 0)
    def _():
        ubuf_ref[0:halo, :] = jnp.zeros((halo, D_MODEL), F32)

    @pl.when(j > 0)
    def _():
        ubuf_ref[0:halo, :] = ubuf_ref[tt:tt + halo, :]


def _mixa_kernel(x_ref, g_ref, win_ref, cw_ref, wout_ref, o_ref, st_ref, h_ref, ubuf_ref, *, tt):
    x = x_ref[...]
    h_ref[...] = _rms(x, g_ref[...]).astype(BF16)
    _carry_halo(ubuf_ref, A_HALO, tt)
    cg = _dot(h_ref[...], win_ref[:, D_MODEL:2 * D_MODEL])
    hv = _dot(h_ref[...], win_ref[:, 2 * D_MODEL:3 * D_MODEL])
    ubuf_ref[A_HALO:A_HALO + tt, :] = cg * hv
    y = cw_ref[A_WIDTH - 1:A_WIDTH, :] * ubuf_ref[A_HALO:A_HALO + tt, :]
    for k in range(A_WIDTH - 1):
        off = A_HALO - (A_WIDTH - 1) + k
        y = y + cw_ref[k:k + 1, :] * ubuf_ref[off:off + tt, :]
    bg = _dot(h_ref[...], win_ref[:, 0:D_MODEL])
    o_ref[...] = x + _dot((bg * y).astype(BF16), wout_ref[...])

    @pl.when(pl.program_id(1) == pl.num_programs(1) - 1)
    def _():
        st_ref[...] = ubuf_ref[A_HALO + tt - (A_WIDTH - 1):A_HALO + tt, :]


def _mixa_prompt(x, g, win, cw, wout):
    n, t, _ = x.shape
    tt = TOK_TILE
    return pl.pallas_call(
        functools.partial(_mixa_kernel, tt=tt),
        out_shape=(jax.ShapeDtypeStruct((n, t, D_MODEL), F32),
                   jax.ShapeDtypeStruct((n, A_WIDTH - 1, D_MODEL), F32)),
        grid=(n, t // tt),
        in_specs=[pl.BlockSpec((None, tt, D_MODEL), lambda i, j: (i, j, 0)),
                  _const_spec((1, D_MODEL)),
                  _const_spec((D_MODEL, 3 * D_MODEL)),
                  _const_spec((A_WIDTH, D_MODEL)),
                  _const_spec((D_MODEL, D_MODEL))],
        out_specs=(pl.BlockSpec((None, tt, D_MODEL), lambda i, j: (i, j, 0)),
                   pl.BlockSpec((None, A_WIDTH - 1, D_MODEL), lambda i, j: (i, 0, 0))),
        scratch_shapes=[pltpu.VMEM((tt, D_MODEL), BF16),
                        pltpu.VMEM((A_HALO + tt, D_MODEL), F32)],
        compiler_params=_params("arbitrary", "arbitrary"),
        name="mix_shortconv",
    )(x, g, win, cw, wout)


def _layer_norm(y, g, b):
    mu = jnp.mean(y, axis=-1, keepdims=True)
    yc = y - mu
    var = jnp.mean(yc * yc, axis=-1, keepdims=True)
    return yc * lax.rsqrt(var + LN_EPS) * g + b


def _dwconv_tile(ubuf_ref, taps_ref, y_ref, r0):
    sub = V7X_SUBLANES
    nblk = CONV_ROWS // sub
    na = -(-B_WIDTH // sub)
    assert B_HALO == na * sub
    rowid = lax.broadcasted_iota(jnp.int32, (sub, V7X_LANES), 0)
    for lt in range(D_MODEL // V7X_LANES):
        lanes = slice(lt * V7X_LANES, (lt + 1) * V7X_LANES)
        ub = [ubuf_ref[pl.ds(r0 + sub * i, sub), lanes] for i in range(nblk + na)]
        acc = None
        for r in range(sub):
            z = []
            for m in range(nblk + 1):
                zm = None
                for a, s in enumerate(range(r, B_WIDTH, sub)):
                    term = taps_ref[s, :, lanes] * ub[na - 1 + m - a]
                    zm = term if zm is None else zm + term
                z.append(zm)
            if r == 0:
                shifted = z[1:]
            else:
                rolled = [pltpu.roll(zm, r, 0) for zm in z]
                shifted = [jnp.where(rowid >= r, rolled[n + 1], rolled[n]) for n in range(nblk)]
            acc = shifted if acc is None else [p + c for p, c in zip(acc, shifted)]
        for n in range(nblk):
            y_ref[pl.ds(r0 + sub * n, sub), lanes] = acc[n]


def _mixb_kernel(x_ref, g_ref, w1_ref, b1_ref, taps_ref, dwb_ref, lng_ref, lnb_ref, w2_ref, b2_ref,
                 o_ref, st_ref, h_ref, ubuf_ref, y_ref, *, tt):
    x = x_ref[...]
    h_ref[...] = _rms(x, g_ref[...]).astype(BF16)
    _carry_halo(ubuf_ref, B_HALO, tt)
    a = _dot(h_ref[...], w1_ref[:, 0:D_MODEL]) + b1_ref[:, 0:D_MODEL]
    gate = _dot(h_ref[...], w1_ref[:, D_MODEL:2 * D_MODEL]) + b1_ref[:, D_MODEL:2 * D_MODEL]
    ubuf_ref[B_HALO:B_HALO + tt, :] = a * jax.nn.sigmoid(gate)

    def conv_body(i, carry):
        _dwconv_tile(ubuf_ref, taps_ref, y_ref, pl.multiple_of(i * CONV_ROWS, CONV_ROWS))
        return carry

    lax.fori_loop(0, tt // CONV_ROWS, conv_body, 0)
    z = _silu(_layer_norm(y_ref[...] + dwb_ref[...], lng_ref[...], lnb_ref[...]))
    o_ref[...] = x + _dot(z.astype(BF16), w2_ref[...]) + b2_ref[...]

    @pl.when(pl.program_id(1) == pl.num_programs(1) - 1)
    def _():
        st_ref[...] = ubuf_ref[B_HALO + tt - (B_WIDTH - 1):B_HALO + tt, :]


def _mixb_prompt(x, g, w1, b1, taps, dwb, lng, lnb, w2, b2):
    n, t, _ = x.shape
    tt = TOK_TILE
    return pl.pallas_call(
        functools.partial(_mixb_kernel, tt=tt),
        out_shape=(jax.ShapeDtypeStruct((n, t, D_MODEL), F32),
                   jax.ShapeDtypeStruct((n, B_WIDTH - 1, D_MODEL), F32)),
        grid=(n, t // tt),
        in_specs=[pl.BlockSpec((None, tt, D_MODEL), lambda i, j: (i, j, 0)),
                  _const_spec((1, D_MODEL)),
                  _const_spec((D_MODEL, 2 * D_MODEL)),
                  _const_spec((1, 2 * D_MODEL)),
                  _const_spec((B_WIDTH, V7X_SUBLANES, D_MODEL)),
                  _const_spec((1, D_MODEL)),
                  _const_spec((1, D_MODEL)),
                  _const_spec((1, D_MODEL)),
                  _const_spec((D_MODEL, D_MODEL)),
                  _const_spec((1, D_MODEL))],
        out_specs=(pl.BlockSpec((None, tt, D_MODEL), lambda i, j: (i, j, 0)),
                   pl.BlockSpec((None, B_WIDTH - 1, D_MODEL), lambda i, j: (i, 0, 0))),
        scratch_shapes=[pltpu.VMEM((tt, D_MODEL), BF16),
                        pltpu.VMEM((B_HALO + tt, D_MODEL), F32),
                        pltpu.VMEM((tt, D_MODEL), F32)],
        compiler_params=_params("arbitrary", "arbitrary"),
        name="mix_conformer",
    )(x, g, w1, b1, taps, dwb, lng, lnb, w2, b2)


def _hgrn_constants():
    c = HG_CHUNK
    t = np.arange(c)[:, None]
    r = np.arange(c)[None, :]
    blocks = [(r <= t), (r > t)]
    masks = [(t == r)]
    for lvl in range(HG_LEVELS):
        h = 1 << lvl
        m = (t // (2 * h)) * (2 * h) + h - 1
        upper = (t & h) != 0
        blocks.append(np.where(upper, (r > m) & (r <= t), (r > t) & (r <= m)))
        masks.append((((t ^ r) >> lvl) == 1) & upper)
    a_stack = np.concatenate(blocks, axis=0).astype(np.float32)
    a_stack = np.concatenate([a_stack, a_stack], axis=1)
    m_stack = np.stack(masks, axis=0).astype(np.float32)
    return jnp.asarray(a_stack, BF16), jnp.asarray(m_stack, F32)


def _gates(fx, lb):
    sig = jax.nn.sigmoid(fx)
    return lb + (1.0 - lb) * sig, (1.0 - lb) * (1.0 - sig)


def _head_out(o, gx, gn):
    o = o * lax.rsqrt(jnp.mean(o * o, axis=-1, keepdims=True) + NORM_EPS) * gn
    return o * _silu(gx)


def _hgrn_head_chunk(q, kk, v, s, e, m_ref):
    c = HG_CHUNK
    e_b = e(0)
    vb = v.astype(BF16)
    o = _dot((q * e_b).astype(BF16), s.astype(BF16))
    scores = _dot_nt(q.astype(BF16), kk.astype(BF16)) * m_ref[0]
    for lvl in range(HG_LEVELS):
        e_l = e(2 + lvl)
        scores = scores + _dot_nt((q * e_l).astype(BF16), (kk * e_l).astype(BF16)) * m_ref[1 + lvl]
    o = o + _dot(scores.astype(BF16), vb)
    decay = jnp.broadcast_to(e_b[c - 1:c, :], (C_DK, C_DK)).T
    s_new = s * decay + _dot_tn((kk * e(1)).astype(BF16), vb)
    return o, s_new


def _mixc_kernel(x_ref, g_ref, wq_ref, lb_ref, gn_ref, wout_ref, a_ref, m_ref,
                 o_ref, sout_ref, h_ref, p_ref, qk_ref, e_ref, om_ref, s_ref, *, tt):
    j = pl.program_id(1)
    d = D_MODEL
    c = HG_CHUNK

    @pl.when(j == 0)
    def _():
        s_ref[...] = jnp.zeros_like(s_ref)

    x = x_ref[...]
    h_ref[...] = _rms(x, g_ref[...]).astype(BF16)
    nb = 4 * V7X_LANES
    for b in range(4 * d // nb):
        p_ref[:, b * nb:(b + 1) * nb] = _dot(h_ref[...], wq_ref[:, b * nb:(b + 1) * nb])

    def prep(ci):
        rows = slice(ci * c, (ci + 1) * c)
        f, kk = _gates(p_ref[rows, d:2 * d], lb_ref[...])
        log2f = jnp.log2(f)
        hi = log2f.astype(BF16)
        lo = (log2f - hi.astype(F32)).astype(BF16)
        e_ref[ci % 2] = jnp.exp2(_dot(a_ref[...], jnp.concatenate([hi, lo], axis=0)))
        qk_ref[ci % 2, :, 0:d] = _silu(p_ref[rows, 0:d]) * (C_DK ** -0.5)
        qk_ref[ci % 2, :, d:2 * d] = kk

    def heads(ci):
        rows = slice(ci * c, (ci + 1) * c)
        slot = ci % 2
        for hd in range(C_HEADS):
            lanes = slice(hd * C_DK, (hd + 1) * C_DK)
            o, s_ref[hd] = _hgrn_head_chunk(
                qk_ref[slot, :, lanes], qk_ref[slot, :, d + hd * C_DK:d + (hd + 1) * C_DK],
                p_ref[rows, 2 * d + hd * C_DV:2 * d + (hd + 1) * C_DV], s_ref[hd],
                lambda r: e_ref[slot, r * c:(r + 1) * c, lanes], m_ref)
            gx = p_ref[rows, 3 * d + hd * C_DV:3 * d + (hd + 1) * C_DV]
            om_ref[rows, lanes] = _head_out(o, gx, gn_ref[...]).astype(BF16)

    nchunk = tt // c
    prep(0)
    for ci in range(nchunk):
        if ci + 1 < nchunk:
            prep(ci + 1)
        heads(ci)
    o_ref[...] = x + _dot(om_ref[...], wout_ref[...])

    @pl.when(j == pl.num_programs(1) - 1)
    def _():
        sout_ref[...] = s_ref[...]


def _mixc_prompt(x, g, wq, lb_row, gn, wout, a_stack, m_stack):
    n, t, _ = x.shape
    tt = TOK_TILE
    return pl.pallas_call(
        functools.partial(_mixc_kernel, tt=tt),
        out_shape=(jax.ShapeDtypeStruct((n, t, D_MODEL), F32),
                   jax.ShapeDtypeStruct((n, C_HEADS, C_DK, C_DV), F32)),
        grid=(n, t // tt),
        in_specs=[pl.BlockSpec((None, tt, D_MODEL), lambda i, j: (i, j, 0)),
                  _const_spec((1, D_MODEL)),
                  _const_spec((D_MODEL, 4 * D_MODEL)),
                  _const_spec((1, D_MODEL)),
                  _const_spec((1, C_DV)),
                  _const_spec((D_MODEL, D_MODEL)),
                  _const_spec(a_stack.shape),
                  _const_spec(m_stack.shape)],
        out_specs=(pl.BlockSpec((None, tt, D_MODEL), lambda i, j: (i, j, 0)),
                   pl.BlockSpec((None, C_HEADS, C_DK, C_DV), lambda i, j: (i, 0, 0, 0))),
        scratch_shapes=[pltpu.VMEM((tt, D_MODEL), BF16),
                        pltpu.VMEM((tt, 4 * D_MODEL), F32),
                        pltpu.VMEM((2, HG_CHUNK, 2 * D_MODEL), F32),
                        pltpu.VMEM((2, (2 + HG_LEVELS) * HG_CHUNK, D_MODEL), F32),
                        pltpu.VMEM((tt, D_MODEL), BF16),
                        pltpu.VMEM((C_HEADS, C_DK, C_DV), F32)],
        compiler_params=_params("arbitrary", "arbitrary"),
        name="mix_hgrn2",
    )(x, g, wq, lb_row, gn, wout, a_stack, m_stack)


def _mixa_sample_kernel(x_ref, st_ref, g_ref, win_ref, cw_ref, wout_ref, o_ref, sto_ref):
    x = x_ref[...]
    h = _rms(x, g_ref[...]).astype(BF16)
    p = _dot(h, win_ref[...])
    bg = p[:, 0:D_MODEL]
    u = p[:, D_MODEL:2 * D_MODEL] * p[:, 2 * D_MODEL:3 * D_MODEL]
    y = cw_ref[0:1, :] * st_ref[0] + cw_ref[1:2, :] * st_ref[1] + cw_ref[2:3, :] * u
    o_ref[...] = x + _dot((bg * y).astype(BF16), wout_ref[...])
    sto_ref[0] = st_ref[1]
    sto_ref[1] = u


def _mixa_sample(x2d, st, g, win, cw, wout):
    n = x2d.shape[0]
    return pl.pallas_call(
        _mixa_sample_kernel,
        out_shape=(jax.ShapeDtypeStruct((n, D_MODEL), F32),
                   jax.ShapeDtypeStruct((A_WIDTH - 1, n, D_MODEL), F32)),
        grid=(1,),
        in_specs=[_const_spec((n, D_MODEL)),
                  _const_spec((A_WIDTH - 1, n, D_MODEL)),
                  _const_spec((1, D_MODEL)),
                  _const_spec((D_MODEL, 3 * D_MODEL)),
                  _const_spec((A_WIDTH, D_MODEL)),
                  _const_spec((D_MODEL, D_MODEL))],
        out_specs=(pl.BlockSpec((n, D_MODEL), lambda i: (0, 0)),
                   pl.BlockSpec((A_WIDTH - 1, n, D_MODEL), lambda i: (0, 0, 0))),
        compiler_params=_params("arbitrary"),
        name="mix_shortconv_sample",
    )(x2d, st, g, win, cw, wout)


def _mixb_sample_kernel(x_ref, st_ref, g_ref, w1_ref, b1_ref, dw_ref, dwb_ref, lng_ref, lnb_ref,
                        w2_ref, b2_ref, o_ref, sto_ref, u_ref, y_ref, *, nb):
    i = pl.program_id(0)
    hist = B_WIDTH - 1

    @pl.when(i == 0)
    def _():
        h = _rms(x_ref[...], g_ref[...]).astype(BF16)
        p = _dot(h, w1_ref[...]) + b1_ref[...]
        u_ref[...] = p[:, 0:D_MODEL] * jax.nn.sigmoid(p[:, D_MODEL:2 * D_MODEL])

    rows = pl.ds(pl.multiple_of(i * nb, nb), nb)
    u = u_ref[rows, :]
    st = st_ref[...]
    y_ref[rows, :] = jnp.sum(st * dw_ref[0:hist, :], axis=1) + dw_ref[hist:hist + 1, :] * u
    sto_ref[:, 0:hist - 1, :] = st[:, 1:hist, :]
    sto_ref[:, hist - 1, :] = u

    @pl.when(i == pl.num_programs(0) - 1)
    def _():
        z = _silu(_layer_norm(y_ref[...] + dwb_ref[...], lng_ref[...], lnb_ref[...]))
        o_ref[...] = x_ref[...] + _dot(z.astype(BF16), w2_ref[...]) + b2_ref[...]


def _mixb_sample(x2d, st, g, w1, b1, dw, dwb, lng, lnb, w2, b2):
    n = x2d.shape[0]
    nb = SAMPLE_BLOCK_B
    hist = B_WIDTH - 1
    return pl.pallas_call(
        functools.partial(_mixb_sample_kernel, nb=nb),
        out_shape=(jax.ShapeDtypeStruct((n, D_MODEL), F32),
                   jax.ShapeDtypeStruct((n, hist, D_MODEL), F32)),
        grid=(n // nb,),
        in_specs=[_const_spec((n, D_MODEL)),
                  pl.BlockSpec((nb, hist, D_MODEL), lambda i: (i, 0, 0)),
                  _const_spec((1, D_MODEL)),
                  _const_spec((D_MODEL, 2 * D_MODEL)),
                  _const_spec((1, 2 * D_MODEL)),
                  _const_spec((B_WIDTH, D_MODEL)),
                  _const_spec((1, D_MODEL)),
                  _const_spec((1, D_MODEL)),
                  _const_spec((1, D_MODEL)),
                  _const_spec((D_MODEL, D_MODEL)),
                  _const_spec((1, D_MODEL))],
        out_specs=(pl.BlockSpec((n, D_MODEL), lambda i: (0, 0)),
                   pl.BlockSpec((nb, hist, D_MODEL), lambda i: (i, 0, 0))),
        scratch_shapes=[pltpu.VMEM((n, D_MODEL), F32),
                        pltpu.VMEM((n, D_MODEL), F32)],
        compiler_params=_params("arbitrary"),
        name="mix_conformer_sample",
    )(x2d, st, g, w1, b1, dw, dwb, lng, lnb, w2, b2)


def _mixc_sample_kernel(x_ref, s_ref, g_ref, wq_ref, lb_ref, gn_ref, wout_ref,
                        o_ref, so_ref, p_ref, orow_ref, om_ref, *, nb):
    i = pl.program_id(0)

    @pl.when(i == 0)
    def _():
        h = _rms(x_ref[...], g_ref[...]).astype(BF16)
        p_ref[...] = _dot(h, wq_ref[...])

    rows = pl.ds(pl.multiple_of(i * nb, nb), nb)
    d = D_MODEL
    f, kk = _gates(p_ref[rows, d:2 * d], lb_ref[...])
    q = _silu(p_ref[rows, 0:d]) * (C_DK ** -0.5)
    v = p_ref[rows, 2 * d:3 * d]

    def column(a, r, lanes):
        return jnp.broadcast_to(a[r:r + 1, lanes], (C_DK, C_DK)).T

    for r in range(nb):
        for hd in range(C_HEADS):
            lanes = slice(hd * C_DK, (hd + 1) * C_DK)
            s_new = column(f, r, lanes) * s_ref[r, hd] + column(kk, r, lanes) * v[r:r + 1, lanes]
            so_ref[r, hd] = s_new
            orow_ref[r:r + 1, lanes] = jnp.sum(column(q, r, lanes) * s_new, axis=0, keepdims=True)

    gx = p_ref[rows, 3 * d:4 * d]
    for hd in range(C_HEADS):
        lanes = slice(hd * C_DK, (hd + 1) * C_DK)
        om_ref[rows, lanes] = _head_out(orow_ref[:, lanes], gx[:, lanes], gn_ref[...])

    @pl.when(i == pl.num_programs(0) - 1)
    def _():
        o_ref[...] = x_ref[...] + _dot(om_ref[...].astype(BF16), wout_ref[...])


def _mixc_sample(x2d, s, g, wq, lb_row, gn, wout):
    n = x2d.shape[0]
    nb = SAMPLE_BLOCK_C
    return pl.pallas_call(
        functools.partial(_mixc_sample_kernel, nb=nb),
        out_shape=(jax.ShapeDtypeStruct((n, D_MODEL), F32),
                   jax.ShapeDtypeStruct((n, C_HEADS, C_DK, C_DV), F32)),
        grid=(n // nb,),
        in_specs=[_const_spec((n, D_MODEL)),
                  pl.BlockSpec((nb, C_HEADS, C_DK, C_DV), lambda i: (i, 0, 0, 0)),
                  _const_spec((1, D_MODEL)),
                  _const_spec((D_MODEL, 4 * D_MODEL)),
                  _const_spec((1, D_MODEL)),
                  _const_spec((1, C_DV)),
                  _const_spec((D_MODEL, D_MODEL))],
        out_specs=(pl.BlockSpec((n, D_MODEL), lambda i: (0, 0)),
                   pl.BlockSpec((nb, C_HEADS, C_DK, C_DV), lambda i: (i, 0, 0, 0))),
        scratch_shapes=[pltpu.VMEM((n, 4 * D_MODEL), F32),
                        pltpu.VMEM((nb, D_MODEL), F32),
                        pltpu.VMEM((n, D_MODEL), F32)],
        compiler_params=_params("arbitrary"),
        name="mix_hgrn2_sample",
    )(x2d, s, g, wq, lb_row, gn, wout)


def _row(a):
    return a.reshape(1, -1)


def kernel(x_prompt, x_sample, state_conva, state_convb, state_hgrn, norm_mix, a_w_in, a_conv_w, a_w_out, b_w_pw1, b_b_pw1, b_dw_w, b_dw_b, b_ln_g, b_ln_b, b_w_pw2, b_b_pw2, c_lower_bounds, c_w_qfig, c_gnorm, c_w_out, norm_ffn, ffn_w_gate_up, ffn_w_down, norm_final):
    nb, t, d = x_prompt.shape
    ns = x_sample.shape[0]
    bf = lambda w: w.astype(BF16)
    a_w_in, a_w_out, b_w_pw1, b_w_pw2 = bf(a_w_in), bf(a_w_out), bf(b_w_pw1), bf(b_w_pw2)
    c_w_qfig, c_w_out, w_gu, w_dn = bf(c_w_qfig), bf(c_w_out), bf(ffn_w_gate_up), bf(ffn_w_down)

    sm = jax.nn.softmax(c_lower_bounds.astype(F32), axis=0)
    lower = jnp.cumsum(sm, axis=0) - sm[0]
    a_stack, m_stack = _hgrn_constants()
    gfin = _row(norm_final)

    xp = x_prompt
    xs = x_sample.reshape(ns, d)
    g_ffn = norm_ffn.reshape(DEPTH, 1, d)
    conva_p, conva_s, convb_p, convb_s, hgrn_p, hgrn_s = [], [], [], [], [], []
    for i in range(DEPTH):
        kind, j = i % 3, i // 3
        g = _row(norm_mix[i])
        if kind == 0:
            w = (g, a_w_in[j], a_conv_w[j], a_w_out[j])
            xp, st = _mixa_prompt(xp, *w)
            conva_p.append(st)
            xs, st = _mixa_sample(xs, jnp.swapaxes(state_conva[j], 0, 1), *w)
            conva_s.append(jnp.swapaxes(st, 0, 1))
        elif kind == 1:
            w = (b_w_pw1[j], _row(b_b_pw1[j]))
            w2 = (_row(b_dw_b[j]), _row(b_ln_g[j]), _row(b_ln_b[j]), b_w_pw2[j], _row(b_b_pw2[j]))
            taps = jnp.broadcast_to(b_dw_w[j][::-1][:, None, :], (B_WIDTH, V7X_SUBLANES, d))
            xp, st = _mixb_prompt(xp, g, *w, taps, *w2)
            convb_p.append(st)
            xs, st = _mixb_sample(xs, state_convb[j], g, *w, b_dw_w[j], *w2)
            convb_s.append(st)
        else:
            gn = _row(c_gnorm[j])
            xp, st = _mixc_prompt(xp, g, c_w_qfig[j], _row(lower[i]), gn, c_w_out[j], a_stack, m_stack)
            hgrn_p.append(st)
            xs, st = _mixc_sample(xs, state_hgrn[j], g, c_w_qfig[j], _row(lower[i]), gn, c_w_out[j])
            hgrn_s.append(st)
        xp, xs = _ffn(xp.reshape(nb * t, d), xs, g_ffn, w_gu, w_dn, gfin, layer=i, final=i == DEPTH - 1)
        xp = xp.reshape(nb, t, d)
    stack = lambda xs_: xs_[0][None] if len(xs_) == 1 else jnp.stack(xs_)
    return (xp, xs.reshape(ns, 1, d), stack(conva_p), stack(conva_s), stack(convb_p),
            stack(convb_s), stack(hgrn_p), stack(hgrn_s))
```

```python
import functools

import numpy as np
import jax
import jax.numpy as jnp
from jax import lax
from jax.experimental import pallas as pl
from jax.experimental.pallas import tpu as pltpu

F32 = jnp.float32
BF16 = jnp.bfloat16

D_MODEL = 1024
DEPTH = 4
D_FF = 2816
A_WIDTH = 3
B_WIDTH = 31
C_HEADS = 8
C_DK = 128
C_DV = 128
NORM_EPS = 1e-6
LN_EPS = 1e-5

V7X_SUBLANES = 8
V7X_LANES = 128
V7X_VMEM_LIMIT_BYTES = 56 * 1024 * 1024

TOK_TILE = 512
FF_CHUNK = 256
FFN_PREP_STEPS = 8
HG_CHUNK = 128
HG_LEVELS = 7
SAMPLE_BLOCK_B = 16
SAMPLE_BLOCK_C = 8
A_HALO = V7X_SUBLANES
B_HALO = 32
CONV_ROWS = 64


def _params(*sem):
    return pltpu.CompilerParams(dimension_semantics=sem,
                                vmem_limit_bytes=V7X_VMEM_LIMIT_BYTES)


def _const_spec(shape):
    nd = len(shape)
    return pl.BlockSpec(shape, lambda *_: (0,) * nd, pipeline_mode=pl.Buffered(1))


def _layer_spec(shape, layer):
    nd = len(shape)
    return pl.BlockSpec((None,) + tuple(shape), lambda *_: (layer,) + (0,) * nd,
                        pipeline_mode=pl.Buffered(1))


def _rms(x, g):
    ms = jnp.mean(x * x, axis=-1, keepdims=True)
    return x * lax.rsqrt(ms + NORM_EPS) * g


def _silu(x):
    return x * jax.nn.sigmoid(x)


def _dot(a, b):
    return jnp.dot(a, b, preferred_element_type=F32)


def _dot_nt(a, b):
    return lax.dot_general(a, b, (((1,), (1,)), ((), ())), preferred_element_type=F32)


def _dot_tn(a, b):
    return lax.dot_general(a, b, (((0,), (0,)), ((), ())), preferred_element_type=F32)


def _ffn_rows(x, g_ref, wgu_ref, wd_ref, gf_ref, h_ref, a_ref, final):
    h_ref[...] = _rms(x, g_ref[...]).astype(BF16)
    for j in range(D_FF // FF_CHUNK):
        lo = j * FF_CHUNK
        gate = _dot(h_ref[...], wgu_ref[:, lo:lo + FF_CHUNK])
        up = _dot(h_ref[...], wgu_ref[:, D_FF + lo:D_FF + lo + FF_CHUNK])
        a_ref[:, lo:lo + FF_CHUNK] = (_silu(gate) * up).astype(BF16)
    y = x + _dot(a_ref[...], wd_ref[...])
    return _rms(y, gf_ref[...]) if final else y


def _ffn_kernel(x_ref, xs_ref, g_ref, wgu_hbm_ref, wd_hbm_ref, gf_ref, o_ref, os_ref,
                wgu_ref, wd_ref, h_ref, a_ref, *, final):
    i = pl.program_id(0)
    args = (g_ref, wgu_ref, wd_ref, gf_ref)

    @pl.when(i < FFN_PREP_STEPS)
    def _():
        ru, rd = wgu_hbm_ref.shape[0], wd_hbm_ref.shape[0]
        wgu_ref[pl.ds(pl.multiple_of(i * ru, ru), ru), :] = wgu_hbm_ref[...].astype(BF16)
        wd_ref[pl.ds(pl.multiple_of(i * rd, rd), rd), :] = wd_hbm_ref[...].astype(BF16)

    @pl.when(i >= FFN_PREP_STEPS)
    def _():
        o_ref[...] = _ffn_rows(x_ref[...], *args, h_ref, a_ref, final)

    @pl.when(i == pl.num_programs(0) - 1)
    def _():
        ns = xs_ref.shape[0]
        os_ref[...] = _ffn_rows(xs_ref[...], *args, h_ref.at[0:ns], a_ref.at[0:ns], final)


def _ffn(x2d, xs, g3, wgu, wd, gf, *, layer, final):
    m = x2d.shape[0]
    ns = xs.shape[0]
    tm = TOK_TILE
    npre = FFN_PREP_STEPS
    ru, rd = D_MODEL // npre, D_FF // npre
    tile = lambda i: (jnp.maximum(i - npre, 0), 0)
    slab = lambda i: (layer, jnp.minimum(i, npre - 1), 0)
    return pl.pallas_call(
        functools.partial(_ffn_kernel, final=final),
        out_shape=(jax.ShapeDtypeStruct((m, D_MODEL), F32),
                   jax.ShapeDtypeStruct((ns, D_MODEL), F32)),
        grid=(npre + m // tm,),
        in_specs=[pl.BlockSpec((tm, D_MODEL), tile),
                  _const_spec((ns, D_MODEL)),
                  _layer_spec((1, D_MODEL), layer),
                  pl.BlockSpec((None, ru, 2 * D_FF), slab),
                  pl.BlockSpec((None, rd, D_MODEL), slab),
                  _const_spec((1, D_MODEL))],
        out_specs=(pl.BlockSpec((tm, D_MODEL), tile),
                   pl.BlockSpec((ns, D_MODEL), lambda i: (0, 0))),
        scratch_shapes=[pltpu.VMEM((D_MODEL, 2 * D_FF), BF16),
                        pltpu.VMEM((D_FF, D_MODEL), BF16),
                        pltpu.VMEM((tm, D_MODEL), BF16),
                        pltpu.VMEM((tm, D_FF), BF16)],
        compiler_params=_params("arbitrary"),
        name="ffn",
    )(x2d, xs, g3, wgu, wd, gf)


def _carry_halo(ubuf_ref, halo, tt):
    j = pl.program_id(1)

    @pl.when(j == 0)
    def _():
        ubuf_ref[0:halo, :] = jnp.zeros((halo, D_MODEL), F32)

    @pl.when(j > 0)
    def _():
        ubuf_ref[0:halo, :] = ubuf_ref[tt:tt + halo, :]


def _mixa_kernel(x_ref, g_ref, win_ref, cw_ref, wout_ref, o_ref, st_ref, h_ref, ubuf_ref, *, tt):
    x = x_ref[...]
    h_ref[...] = _rms(x, g_ref[...]).astype(BF16)
    _carry_halo(ubuf_ref, A_HALO, tt)
    cg = _dot(h_ref[...], win_ref[:, D_MODEL:2 * D_MODEL])
    hv = _dot(h_ref[...], win_ref[:, 2 * D_MODEL:3 * D_MODEL])
    ubuf_ref[A_HALO:A_HALO + tt, :] = cg * hv
    y = cw_ref[A_WIDTH - 1:A_WIDTH, :] * ubuf_ref[A_HALO:A_HALO + tt, :]
    for k in range(A_WIDTH - 1):
        off = A_HALO - (A_WIDTH - 1) + k
        y = y + cw_ref[k:k + 1, :] * ubuf_ref[off:off + tt, :]
    bg = _dot(h_ref[...], win_ref[:, 0:D_MODEL])
    o_ref[...] = x + _dot((bg * y).astype(BF16), wout_ref[...])

    @pl.when(pl.program_id(1) == pl.num_programs(1) - 1)
    def _():
        st_ref[...] = ubuf_ref[A_HALO + tt - (A_WIDTH - 1):A_HALO + tt, :]


def _mixa_prompt(x, g, win, cw, wout):
    n, t, _ = x.shape
    tt = TOK_TILE
    return pl.pallas_call(
        functools.partial(_mixa_kernel, tt=tt),
        out_shape=(jax.ShapeDtypeStruct((n, t, D_MODEL), F32),
                   jax.ShapeDtypeStruct((n, A_WIDTH - 1, D_MODEL), F32)),
        grid=(n, t // tt),
        in_specs=[pl.BlockSpec((None, tt, D_MODEL), lambda i, j: (i, j, 0)),
                  _const_spec((1, D_MODEL)),
                  _const_spec((D_MODEL, 3 * D_MODEL)),
                  _const_spec((A_WIDTH, D_MODEL)),
                  _const_spec((D_MODEL, D_MODEL))],
        out_specs=(pl.BlockSpec((None, tt, D_MODEL), lambda i, j: (i, j, 0)),
                   pl.BlockSpec((None, A_WIDTH - 1, D_MODEL), lambda i, j: (i, 0, 0))),
        scratch_shapes=[pltpu.VMEM((tt, D_MODEL), BF16),
                        pltpu.VMEM((A_HALO + tt, D_MODEL), F32)],
        compiler_params=_params("arbitrary", "arbitrary"),
        name="mix_shortconv",
    )(x, g, win, cw, wout)


def _layer_norm(y, g, b):
    mu = jnp.mean(y, axis=-1, keepdims=True)
    yc = y - mu
    var = jnp.mean(yc * yc, axis=-1, keepdims=True)
    return yc * lax.rsqrt(var + LN_EPS) * g + b


def _dwconv_tile(ubuf_ref, taps_ref, y_ref, r0):
    sub = V7X_SUBLANES
    nblk = CONV_ROWS // sub
    na = -(-B_WIDTH // sub)
    assert B_HALO == na * sub
    rowid = lax.broadcasted_iota(jnp.int32, (sub, V7X_LANES), 0)
    for lt in range(D_MODEL // V7X_LANES):
        lanes = slice(lt * V7X_LANES, (lt + 1) * V7X_LANES)
        ub = [ubuf_ref[pl.ds(r0 + sub * i, sub), lanes] for i in range(nblk + na)]
        acc = None
        for r in range(sub):
            z = []
            for m in range(nblk + 1):
                zm = None
                for a, s in enumerate(range(r, B_WIDTH, sub)):
                    term = taps_ref[s, :, lanes] * ub[na - 1 + m - a]
                    zm = term if zm is None else zm + term
                z.append(zm)
            if r == 0:
                shifted = z[1:]
            else:
                rolled = [pltpu.roll(zm, r, 0) for zm in z]
                shifted = [jnp.where(rowid >= r, rolled[n + 1], rolled[n]) for n in range(nblk)]
            acc = shifted if acc is None else [p + c for p, c in zip(acc, shifted)]
        for n in range(nblk):
            y_ref[pl.ds(r0 + sub * n, sub), lanes] = acc[n]


def _mixb_kernel(x_ref, g_ref, w1_ref, b1_ref, taps_ref, dwb_ref, lng_ref, lnb_ref, w2_ref, b2_ref,
                 o_ref, st_ref, h_ref, ubuf_ref, y_ref, *, tt):
    x = x_ref[...]
    h_ref[...] = _rms(x, g_ref[...]).astype(BF16)
    _carry_halo(ubuf_ref, B_HALO, tt)
    a = _dot(h_ref[...], w1_ref[:, 0:D_MODEL]) + b1_ref[:, 0:D_MODEL]
    gate = _dot(h_ref[...], w1_ref[:, D_MODEL:2 * D_MODEL]) + b1_ref[:, D_MODEL:2 * D_MODEL]
    ubuf_ref[B_HALO:B_HALO + tt, :] = a * jax.nn.sigmoid(gate)

    def conv_body(i, carry):
        _dwconv_tile(ubuf_ref, taps_ref, y_ref, pl.multiple_of(i * CONV_ROWS, CONV_ROWS))
        return carry

    lax.fori_loop(0, tt // CONV_ROWS, conv_body, 0)
    z = _silu(_layer_norm(y_ref[...] + dwb_ref[...], lng_ref[...], lnb_ref[...]))
    o_ref[...] = x + _dot(z.astype(BF16), w2_ref[...]) + b2_ref[...]

    @pl.when(pl.program_id(1) == pl.num_programs(1) - 1)
    def _():
        st_ref[...] = ubuf_ref[B_HALO + tt - (B_WIDTH - 1):B_HALO + tt, :]


def _mixb_prompt(x, g, w1, b1, taps, dwb, lng, lnb, w2, b2):
    n, t, _ = x.shape
    tt = TOK_TILE
    return pl.pallas_call(
        functools.partial(_mixb_kernel, tt=tt),
        out_shape=(jax.ShapeDtypeStruct((n, t, D_MODEL), F32),
                   jax.ShapeDtypeStruct((n, B_WIDTH - 1, D_MODEL), F32)),
        grid=(n, t // tt),
        in_specs=[pl.BlockSpec((None, tt, D_MODEL), lambda i, j: (i, j, 0)),
                  _const_spec((1, D_MODEL)),
                  _const_spec((D_MODEL, 2 * D_MODEL)),
                  _const_spec((1, 2 * D_MODEL)),
                  _const_spec((B_WIDTH, V7X_SUBLANES, D_MODEL)),
                  _const_spec((1, D_MODEL)),
                  _const_spec((1, D_MODEL)),
                  _const_spec((1, D_MODEL)),
                  _const_spec((D_MODEL, D_MODEL)),
                  _const_spec((1, D_MODEL))],
        out_specs=(pl.BlockSpec((None, tt, D_MODEL), lambda i, j: (i, j, 0)),
                   pl.BlockSpec((None, B_WIDTH - 1, D_MODEL), lambda i, j: (i, 0, 0))),
        scratch_shapes=[pltpu.VMEM((tt, D_MODEL), BF16),
                        pltpu.VMEM((B_HALO + tt, D_MODEL), F32),
                        pltpu.VMEM((tt, D_MODEL), F32)],
        compiler_params=_params("arbitrary", "arbitrary"),
        name="mix_conformer",
    )(x, g, w1, b1, taps, dwb, lng, lnb, w2, b2)


def _hgrn_constants():
    c = HG_CHUNK
    t = np.arange(c)[:, None]
    r = np.arange(c)[None, :]
    blocks = [(r <= t), (r > t)]
    masks = [(t == r)]
    for lvl in range(HG_LEVELS):
        h = 1 << lvl
        m = (t // (2 * h)) * (2 * h) + h - 1
        upper = (t & h) != 0
        blocks.append(np.where(upper, (r > m) & (r <= t), (r > t) & (r <= m)))
        masks.append((((t ^ r) >> lvl) == 1) & upper)
    a_stack = np.concatenate(blocks, axis=0).astype(np.float32)
    a_stack = np.concatenate([a_stack, a_stack], axis=1)
    m_stack = np.stack(masks, axis=0).astype(np.float32)
    return jnp.asarray(a_stack, BF16), jnp.asarray(m_stack, F32)


def _gates(fx, lb):
    sig = jax.nn.sigmoid(fx)
    return lb + (1.0 - lb) * sig, (1.0 - lb) * (1.0 - sig)


def _head_out(o, gx, gn):
    o = o * lax.rsqrt(jnp.mean(o * o, axis=-1, keepdims=True) + NORM_EPS) * gn
    return o * _silu(gx)


def _hgrn_head_chunk(q, kk, v, s, e, m_ref):
    c = HG_CHUNK
    e_b = e(0)
    vb = v.astype(BF16)
    o = _dot((q * e_b).astype(BF16), s.astype(BF16))
    scores = _dot_nt(q.astype(BF16), kk.astype(BF16)) * m_ref[0]
    for lvl in range(HG_LEVELS):
        e_l = e(2 + lvl)
        scores = scores + _dot_nt((q * e_l).astype(BF16), (kk * e_l).astype(BF16)) * m_ref[1 + lvl]
    o = o + _dot(scores.astype(BF16), vb)
    decay = jnp.broadcast_to(e_b[c - 1:c, :], (C_DK, C_DK)).T
    s_new = s * decay + _dot_tn((kk * e(1)).astype(BF16), vb)
    return o, s_new


def _mixc_kernel(x_ref, g_ref, wq_ref, lb_ref, gn_ref, wout_ref, a_ref, m_ref,
                 o_ref, sout_ref, h_ref, p_ref, qk_ref, e_ref, om_ref, s_ref, *, tt):
    j = pl.program_id(1)
    d = D_MODEL
    c = HG_CHUNK

    @pl.when(j == 0)
    def _():
        s_ref[...] = jnp.zeros_like(s_ref)

    x = x_ref[...]
    h_ref[...] = _rms(x, g_ref[...]).astype(BF16)
    nb = 4 * V7X_LANES
    for b in range(4 * d // nb):
        p_ref[:, b * nb:(b + 1) * nb] = _dot(h_ref[...], wq_ref[:, b * nb:(b + 1) * nb])

    def prep(ci):
        rows = slice(ci * c, (ci + 1) * c)
        f, kk = _gates(p_ref[rows, d:2 * d], lb_ref[...])
        log2f = jnp.log2(f)
        hi = log2f.astype(BF16)
        lo = (log2f - hi.astype(F32)).astype(BF16)
        e_ref[ci % 2] = jnp.exp2(_dot(a_ref[...], jnp.concatenate([hi, lo], axis=0)))
        qk_ref[ci % 2, :, 0:d] = _silu(p_ref[rows, 0:d]) * (C_DK ** -0.5)
        qk_ref[ci % 2, :, d:2 * d] = kk

    def heads(ci):
        rows = slice(ci * c, (ci + 1) * c)
        slot = ci % 2
        for hd in range(C_HEADS):
            lanes = slice(hd * C_DK, (hd + 1) * C_DK)
            o, s_ref[hd] = _hgrn_head_chunk(
                qk_ref[slot, :, lanes], qk_ref[slot, :, d + hd * C_DK:d + (hd + 1) * C_DK],
                p_ref[rows, 2 * d + hd * C_DV:2 * d + (hd + 1) * C_DV], s_ref[hd],
                lambda r: e_ref[slot, r * c:(r + 1) * c, lanes], m_ref)
            gx = p_ref[rows, 3 * d + hd * C_DV:3 * d + (hd + 1) * C_DV]
            om_ref[rows, lanes] = _head_out(o, gx, gn_ref[...]).astype(BF16)

    nchunk = tt // c
    prep(0)
    for ci in range(nchunk):
        if ci + 1 < nchunk:
            prep(ci + 1)
        heads(ci)
    o_ref[...] = x + _dot(om_ref[...], wout_ref[...])

    @pl.when(j == pl.num_programs(1) - 1)
    def _():
        sout_ref[...] = s_ref[...]


def _mixc_prompt(x, g, wq, lb_row, gn, wout, a_stack, m_stack):
    n, t, _ = x.shape
    tt = TOK_TILE
    return pl.pallas_call(
        functools.partial(_mixc_kernel, tt=tt),
        out_shape=(jax.ShapeDtypeStruct((n, t, D_MODEL), F32),
                   jax.ShapeDtypeStruct((n, C_HEADS, C_DK, C_DV), F32)),
        grid=(n, t // tt),
        in_specs=[pl.BlockSpec((None, tt, D_MODEL), lambda i, j: (i, j, 0)),
                  _const_spec((1, D_MODEL)),
                  _const_spec((D_MODEL, 4 * D_MODEL)),
                  _const_spec((1, D_MODEL)),
                  _const_spec((1, C_DV)),
                  _const_spec((D_MODEL, D_MODEL)),
                  _const_spec(a_stack.shape),
                  _const_spec(m_stack.shape)],
        out_specs=(pl.BlockSpec((None, tt, D_MODEL), lambda i, j: (i, j, 0)),
                   pl.BlockSpec((None, C_HEADS, C_DK, C_DV), lambda i, j: (i, 0, 0, 0))),
        scratch_shapes=[pltpu.VMEM((tt, D_MODEL), BF16),
                        pltpu.VMEM((tt, 4 * D_MODEL), F32),
                        pltpu.VMEM((2, HG_CHUNK, 2 * D_MODEL), F32),
                        pltpu.VMEM((2, (2 + HG_LEVELS) * HG_CHUNK, D_MODEL), F32),
                        pltpu.VMEM((tt, D_MODEL), BF16),
                        pltpu.VMEM((C_HEADS, C_DK, C_DV), F32)],
        compiler_params=_params("arbitrary", "arbitrary"),
        name="mix_hgrn2",
    )(x, g, wq, lb_row, gn, wout, a_stack, m_stack)


def _mixa_sample_kernel(x_ref, st_ref, g_ref, win_ref, cw_ref, wout_ref, o_ref, sto_ref):
    x = x_ref[...]
    h = _rms(x, g_ref[...]).astype(BF16)
    p = _dot(h, win_ref[...])
    bg = p[:, 0:D_MODEL]
    u = p[:, D_MODEL:2 * D_MODEL] * p[:, 2 * D_MODEL:3 * D_MODEL]
    y = cw_ref[0:1, :] * st_ref[0] + cw_ref[1:2, :] * st_ref[1] + cw_ref[2:3, :] * u
    o_ref[...] = x + _dot((bg * y).astype(BF16), wout_ref[...])
    sto_ref[0] = st_ref[1]
    sto_ref[1] = u


def _mixa_sample(x2d, st, g, win, cw, wout):
    n = x2d.shape[0]
    return pl.pallas_call(
        _mixa_sample_kernel,
        out_shape=(jax.ShapeDtypeStruct((n, D_MODEL), F32),
                   jax.ShapeDtypeStruct((A_WIDTH - 1, n, D_MODEL), F32)),
        grid=(1,),
        in_specs=[_const_spec((n, D_MODEL)),
                  _const_spec((A_WIDTH - 1, n, D_MODEL)),
                  _const_spec((1, D_MODEL)),
                  _const_spec((D_MODEL, 3 * D_MODEL)),
                  _const_spec((A_WIDTH, D_MODEL)),
                  _const_spec((D_MODEL, D_MODEL))],
        out_specs=(pl.BlockSpec((n, D_MODEL), lambda i: (0, 0)),
                   pl.BlockSpec((A_WIDTH - 1, n, D_MODEL), lambda i: (0, 0, 0))),
        compiler_params=_params("arbitrary"),
        name="mix_shortconv_sample",
    )(x2d, st, g, win, cw, wout)


def _mixb_sample_kernel(x_ref, st_ref, g_ref, w1_ref, b1_ref, dw_ref, dwb_ref, lng_ref, lnb_ref,
                        w2_ref, b2_ref, o_ref, sto_ref, u_ref, y_ref, *, nb):
    i = pl.program_id(0)
    hist = B_WIDTH - 1

    @pl.when(i == 0)
    def _():
        h = _rms(x_ref[...], g_ref[...]).astype(BF16)
        p = _dot(h, w1_ref[...]) + b1_ref[...]
        u_ref[...] = p[:, 0:D_MODEL] * jax.nn.sigmoid(p[:, D_MODEL:2 * D_MODEL])

    rows = pl.ds(pl.multiple_of(i * nb, nb), nb)
    u = u_ref[rows, :]
    st = st_ref[...]
    y_ref[rows, :] = jnp.sum(st * dw_ref[0:hist, :], axis=1) + dw_ref[hist:hist + 1, :] * u
    sto_ref[:, 0:hist - 1, :] = st[:, 1:hist, :]
    sto_ref[:, hist - 1, :] = u

    @pl.when(i == pl.num_programs(0) - 1)
    def _():
        z = _silu(_layer_norm(y_ref[...] + dwb_ref[...], lng_ref[...], lnb_ref[...]))
        o_ref[...] = x_ref[...] + _dot(z.astype(BF16), w2_ref[...]) + b2_ref[...]


def _mixb_sample(x2d, st, layer, g, w1, b1, dw, dwb, lng, lnb, w2, b2):
    n = x2d.shape[0]
    nb = SAMPLE_BLOCK_B
    hist = B_WIDTH - 1
    return pl.pallas_call(
        functools.partial(_mixb_sample_kernel, nb=nb),
        out_shape=(jax.ShapeDtypeStruct((n, D_MODEL), F32),
                   jax.ShapeDtypeStruct((n, hist, D_MODEL), F32)),
        grid=(n // nb,),
        in_specs=[_const_spec((n, D_MODEL)),
                  pl.BlockSpec((None, nb, hist, D_MODEL), lambda i: (layer, i, 0, 0)),
                  _const_spec((1, D_MODEL)),
                  _const_spec((D_MODEL, 2 * D_MODEL)),
                  _const_spec((1, 2 * D_MODEL)),
                  _const_spec((B_WIDTH, D_MODEL)),
                  _const_spec((1, D_MODEL)),
                  _const_spec((1, D_MODEL)),
                  _const_spec((1, D_MODEL)),
                  _const_spec((D_MODEL, D_MODEL)),
                  _const_spec((1, D_MODEL))],
        out_specs=(pl.BlockSpec((n, D_MODEL), lambda i: (0, 0)),
                   pl.BlockSpec((nb, hist, D_MODEL), lambda i: (i, 0, 0))),
        scratch_shapes=[pltpu.VMEM((n, D_MODEL), F32),
                        pltpu.VMEM((n, D_MODEL), F32)],
        compiler_params=_params("arbitrary"),
        name="mix_conformer_sample",
    )(x2d, st, g, w1, b1, dw, dwb, lng, lnb, w2, b2)


def _sample_selector():
    nb = SAMPLE_BLOCK_C
    sel = np.zeros((4 * nb, nb * C_DV), np.float32)
    for p in range(3):
        for r in range(nb):
            sel[p * nb + r, r * C_DV:(r + 1) * C_DV] = 1.0
    return jnp.asarray(sel, BF16)


def _split3_rows(a):
    hi = a.astype(BF16).astype(F32)
    r1 = a - hi
    mid = r1.astype(BF16).astype(F32)
    lo = r1 - mid
    return jnp.concatenate([hi, mid, lo, jnp.zeros_like(a)], axis=0).astype(BF16)


def _mixc_sample_kernel(x_ref, s_ref, g_ref, wq_ref, lb_ref, gn_ref, wout_ref, sel_ref,
                        o_ref, so_ref, p_ref, orow_ref, om_ref, *, nb):
    i = pl.program_id(0)

    @pl.when(i == 0)
    def _():
        h = _rms(x_ref[...], g_ref[...]).astype(BF16)
        p_ref[...] = _dot(h, wq_ref[...])

    rows = pl.ds(pl.multiple_of(i * nb, nb), nb)
    d = D_MODEL
    f, kk = _gates(p_ref[rows, d:2 * d], lb_ref[...])
    q = _silu(p_ref[rows, 0:d]) * (C_DK ** -0.5)
    v = p_ref[rows, 2 * d:3 * d]

    for hd in range(C_HEADS):
        lanes = slice(hd * C_DK, (hd + 1) * C_DK)
        parts = jnp.concatenate([_split3_rows(a[:, lanes]) for a in (f, kk, q)], axis=1)
        cols = _dot_tn(parts, sel_ref[...])
        for r in range(nb):
            blk = slice(r * C_DV, (r + 1) * C_DV)
            s_new = cols[0:C_DK, blk] * s_ref[r, hd] + cols[C_DK:2 * C_DK, blk] * v[r:r + 1, lanes]
            so_ref[r, hd] = s_new
            orow_ref[r:r + 1, lanes] = jnp.sum(cols[2 * C_DK:3 * C_DK, blk] * s_new, axis=0,
                                               keepdims=True)

    gx = p_ref[rows, 3 * d:4 * d]
    for hd in range(C_HEADS):
        lanes = slice(hd * C_DK, (hd + 1) * C_DK)
        om_ref[rows, lanes] = _head_out(orow_ref[:, lanes], gx[:, lanes], gn_ref[...])

    @pl.when(i == pl.num_programs(0) - 1)
    def _():
        o_ref[...] = x_ref[...] + _dot(om_ref[...].astype(BF16), wout_ref[...])


def _mixc_sample(x2d, s, g, wq, lb_row, gn, wout, sel):
    n = x2d.shape[0]
    nb = SAMPLE_BLOCK_C
    return pl.pallas_call(
        functools.partial(_mixc_sample_kernel, nb=nb),
        out_shape=(jax.ShapeDtypeStruct((n, D_MODEL), F32),
                   jax.ShapeDtypeStruct((n, C_HEADS, C_DK, C_DV), F32)),
        grid=(n // nb,),
        in_specs=[_const_spec((n, D_MODEL)),
                  pl.BlockSpec((nb, C_HEADS, C_DK, C_DV), lambda i: (i, 0, 0, 0)),
                  _const_spec((1, D_MODEL)),
                  _const_spec((D_MODEL, 4 * D_MODEL)),
                  _const_spec((1, D_MODEL)),
                  _const_spec((1, C_DV)),
                  _const_spec((D_MODEL, D_MODEL)),
                  _const_spec(sel.shape)],
        out_specs=(pl.BlockSpec((n, D_MODEL), lambda i: (0, 0)),
                   pl.BlockSpec((nb, C_HEADS, C_DK, C_DV), lambda i: (i, 0, 0, 0))),
        scratch_shapes=[pltpu.VMEM((n, 4 * D_MODEL), F32),
                        pltpu.VMEM((nb, D_MODEL), F32),
                        pltpu.VMEM((n, D_MODEL), F32)],
        compiler_params=_params("arbitrary"),
        name="mix_hgrn2_sample",
    )(x2d, s, g, wq, lb_row, gn, wout, sel)


def _row(a):
    return a.reshape(1, -1)


def kernel(x_prompt, x_sample, state_conva, state_convb, state_hgrn, norm_mix, a_w_in, a_conv_w, a_w_out, b_w_pw1, b_b_pw1, b_dw_w, b_dw_b, b_ln_g, b_ln_b, b_w_pw2, b_b_pw2, c_lower_bounds, c_w_qfig, c_gnorm, c_w_out, norm_ffn, ffn_w_gate_up, ffn_w_down, norm_final):
    nb, t, d = x_prompt.shape
    ns = x_sample.shape[0]
    bf = lambda w: w.astype(BF16)
    a_w_in, a_w_out, b_w_pw1, b_w_pw2 = bf(a_w_in), bf(a_w_out), bf(b_w_pw1), bf(b_w_pw2)
    c_w_qfig, c_w_out = bf(c_w_qfig), bf(c_w_out)

    sm = jax.nn.softmax(c_lower_bounds.astype(F32), axis=0)
    lower = jnp.cumsum(sm, axis=0) - sm[0]
    a_stack, m_stack = _hgrn_constants()
    gfin = _row(norm_final)

    xp = x_prompt
    xs = x_sample.reshape(ns, d)
    g_ffn = norm_ffn.reshape(DEPTH, 1, d)
    conva_p, conva_s, convb_p, convb_s, hgrn_p, hgrn_s = [], [], [], [], [], []
    for i in range(DEPTH):
        kind, j = i % 3, i // 3
        g = _row(norm_mix[i])
        if kind == 0:
            w = (g, a_w_in[j], a_conv_w[j], a_w_out[j])
            xp, st = _mixa_prompt(xp, *w)
            conva_p.append(st)
            xs, st = _mixa_sample(xs, jnp.swapaxes(state_conva[j], 0, 1), *w)
            conva_s.append(jnp.swapaxes(st, 0, 1))
        elif kind == 1:
            w = (b_w_pw1[j], _row(b_b_pw1[j]))
            w2 = (_row(b_dw_b[j]), _row(b_ln_g[j]), _row(b_ln_b[j]), b_w_pw2[j], _row(b_b_pw2[j]))
            taps = jnp.broadcast_to(b_dw_w[j][::-1][:, None, :], (B_WIDTH, V7X_SUBLANES, d))
            xp, st = _mixb_prompt(xp, g, *w, taps, *w2)
            convb_p.append(st)
            xs, st = _mixb_sample(xs, state_convb, j, g, *w, b_dw_w[j], *w2)
            convb_s.append(st)
        else:
            gn = _row(c_gnorm[j])
            xp, st = _mixc_prompt(xp, g, c_w_qfig[j], _row(lower[i]), gn, c_w_out[j], a_stack, m_stack)
            hgrn_p.append(st)
            xs, st = _mixc_sample(xs, state_hgrn[j], g, c_w_qfig[j], _row(lower[i]), gn, c_w_out[j],
                                  _sample_selector())
            hgrn_s.append(st)
        xp, xs = _ffn(xp.reshape(nb * t, d), xs, g_ffn, ffn_w_gate_up, ffn_w_down, gfin, layer=i,
                      final=i == DEPTH - 1)
        xp = xp.reshape(nb, t, d)
    stack = lambda xs_: xs_[0][None] if len(xs_) == 1 else jnp.stack(xs_)
    return (xp, xs.reshape(ns, 1, d), stack(conva_p), stack(conva_s), stack(convb_p),
            stack(convb_s), stack(hgrn_p), stack(hgrn_s))
```

```python
import functools

import numpy as np
import jax
import jax.numpy as jnp
from jax import lax
from jax.experimental import pallas as pl
from jax.experimental.pallas import tpu as pltpu

F32 = jnp.float32
BF16 = jnp.bfloat16

D_MODEL = 1024
DEPTH = 4
D_FF = 2816
A_WIDTH = 3
B_WIDTH = 31
C_HEADS = 8
C_DK = 128
C_DV = 128
NORM_EPS = 1e-6
LN_EPS = 1e-5

V7X_SUBLANES = 8
V7X_LANES = 128
V7X_VMEM_LIMIT_BYTES = 56 * 1024 * 1024

TOK_TILE = 512
FF_CHUNK = 256
FFN_PREP_STEPS = 8
HG_CHUNK = 128
HG_LEVELS = 7
SAMPLE_BLOCK_B = 16
SAMPLE_BLOCK_C = 8
A_HALO = V7X_SUBLANES
B_HALO = 32
CONV_ROWS = 64


def _params(*sem):
    return pltpu.CompilerParams(dimension_semantics=sem,
                                vmem_limit_bytes=V7X_VMEM_LIMIT_BYTES)


def _const_spec(shape):
    nd = len(shape)
    return pl.BlockSpec(shape, lambda *_: (0,) * nd, pipeline_mode=pl.Buffered(1))


def _layer_spec(shape, layer):
    nd = len(shape)
    return pl.BlockSpec((None,) + tuple(shape), lambda *_: (layer,) + (0,) * nd,
                        pipeline_mode=pl.Buffered(1))


def _rms(x, g):
    ms = jnp.mean(x * x, axis=-1, keepdims=True)
    return x * lax.rsqrt(ms + NORM_EPS) * g


def _silu(x):
    return x * jax.nn.sigmoid(x)


def _dot(a, b):
    return jnp.dot(a, b, preferred_element_type=F32)


def _dot_nt(a, b):
    return lax.dot_general(a, b, (((1,), (1,)), ((), ())), preferred_element_type=F32)


def _dot_tn(a, b):
    return lax.dot_general(a, b, (((0,), (0,)), ((), ())), preferred_element_type=F32)


def _ffn_rows(x, g_ref, wgu_ref, wd_ref, gf_ref, h_ref, a_ref, final):
    h_ref[...] = _rms(x, g_ref[...]).astype(BF16)
    for j in range(D_FF // FF_CHUNK):
        lo = j * FF_CHUNK
        gate = _dot(h_ref[...], wgu_ref[:, lo:lo + FF_CHUNK])
        up = _dot(h_ref[...], wgu_ref[:, D_FF + lo:D_FF + lo + FF_CHUNK])
        a_ref[:, lo:lo + FF_CHUNK] = (_silu(gate) * up).astype(BF16)
    y = x + _dot(a_ref[...], wd_ref[...])
    return _rms(y, gf_ref[...]) if final else y


def _ffn_kernel(x_ref, xs_ref, g_ref, wgu_hbm_ref, wd_hbm_ref, gf_ref, o_ref, os_ref,
                wgu_ref, wd_ref, h_ref, a_ref, *, final):
    i = pl.program_id(0)
    args = (g_ref, wgu_ref, wd_ref, gf_ref)

    @pl.when(i < FFN_PREP_STEPS)
    def _():
        ru, rd = wgu_hbm_ref.shape[0], wd_hbm_ref.shape[0]
        wgu_ref[pl.ds(pl.multiple_of(i * ru, ru), ru), :] = wgu_hbm_ref[...].astype(BF16)
        wd_ref[pl.ds(pl.multiple_of(i * rd, rd), rd), :] = wd_hbm_ref[...].astype(BF16)

    @pl.when(i >= FFN_PREP_STEPS)
    def _():
        o_ref[...] = _ffn_rows(x_ref[...], *args, h_ref, a_ref, final)

    @pl.when(i == pl.num_programs(0) - 1)
    def _():
        ns = xs_ref.shape[0]
        os_ref[...] = _ffn_rows(xs_ref[...], *args, h_ref.at[0:ns], a_ref.at[0:ns], final)


def _ffn(x2d, xs, g3, wgu, wd, gf, *, layer, final):
    m = x2d.shape[0]
    ns = xs.shape[0]
    tm = TOK_TILE
    npre = FFN_PREP_STEPS
    ru, rd = D_MODEL // npre, D_FF // npre
    tile = lambda i: (jnp.maximum(i - npre, 0), 0)
    slab = lambda i: (layer, jnp.minimum(i, npre - 1), 0)
    return pl.pallas_call(
        functools.partial(_ffn_kernel, final=final),
        out_shape=(jax.ShapeDtypeStruct((m, D_MODEL), F32),
                   jax.ShapeDtypeStruct((ns, D_MODEL), F32)),
        grid=(npre + m // tm,),
        in_specs=[pl.BlockSpec((tm, D_MODEL), tile),
                  _const_spec((ns, D_MODEL)),
                  _layer_spec((1, D_MODEL), layer),
                  pl.BlockSpec((None, ru, 2 * D_FF), slab),
                  pl.BlockSpec((None, rd, D_MODEL), slab),
                  _const_spec((1, D_MODEL))],
        out_specs=(pl.BlockSpec((tm, D_MODEL), tile),
                   pl.BlockSpec((ns, D_MODEL), lambda i: (0, 0))),
        scratch_shapes=[pltpu.VMEM((D_MODEL, 2 * D_FF), BF16),
                        pltpu.VMEM((D_FF, D_MODEL), BF16),
                        pltpu.VMEM((tm, D_MODEL), BF16),
                        pltpu.VMEM((tm, D_FF), BF16)],
        compiler_params=_params("arbitrary"),
        name="ffn",
    )(x2d, xs, g3, wgu, wd, gf)


def _carry_halo(ubuf_ref, halo, tt):
    j = pl.program_id(1)

    @pl.when(j == 0)
    def _():
        ubuf_ref[0:halo, :] = jnp.zeros((halo, D_MODEL), F32)

    @pl.when(j > 0)
    def _():
        ubuf_ref[0:halo, :] = ubuf_ref[tt:tt + halo, :]


def _mixa_kernel(x_ref, g_ref, win_ref, cw_ref, wout_ref, o_ref, st_ref, h_ref, ubuf_ref, *, tt):
    x = x_ref[...]
    h_ref[...] = _rms(x, g_ref[...]).astype(BF16)
    _carry_halo(ubuf_ref, A_HALO, tt)
    cg = _dot(h_ref[...], win_ref[:, D_MODEL:2 * D_MODEL])
    hv = _dot(h_ref[...], win_ref[:, 2 * D_MODEL:3 * D_MODEL])
    ubuf_ref[A_HALO:A_HALO + tt, :] = cg * hv
    y = cw_ref[A_WIDTH - 1:A_WIDTH, :] * ubuf_ref[A_HALO:A_HALO + tt, :]
    for k in range(A_WIDTH - 1):
        off = A_HALO - (A_WIDTH - 1) + k
        y = y + cw_ref[k:k + 1, :] * ubuf_ref[off:off + tt, :]
    bg = _dot(h_ref[...], win_ref[:, 0:D_MODEL])
    o_ref[...] = x + _dot((bg * y).astype(BF16), wout_ref[...])

    @pl.when(pl.program_id(1) == pl.num_programs(1) - 1)
    def _():
        st_ref[...] = ubuf_ref[A_HALO + tt - (A_WIDTH - 1):A_HALO + tt, :]


def _mixa_prompt(x, g, win, cw, wout):
    n, t, _ = x.shape
    tt = TOK_TILE
    return pl.pallas_call(
        functools.partial(_mixa_kernel, tt=tt),
        out_shape=(jax.ShapeDtypeStruct((n, t, D_MODEL), F32),
                   jax.ShapeDtypeStruct((n, A_WIDTH - 1, D_MODEL), F32)),
        grid=(n, t // tt),
        in_specs=[pl.BlockSpec((None, tt, D_MODEL), lambda i, j: (i, j, 0)),
                  _const_spec((1, D_MODEL)),
                  _const_spec((D_MODEL, 3 * D_MODEL)),
                  _const_spec((A_WIDTH, D_MODEL)),
                  _const_spec((D_MODEL, D_MODEL))],
        out_specs=(pl.BlockSpec((None, tt, D_MODEL), lambda i, j: (i, j, 0)),
                   pl.BlockSpec((None, A_WIDTH - 1, D_MODEL), lambda i, j: (i, 0, 0))),
        scratch_shapes=[pltpu.VMEM((tt, D_MODEL), BF16),
                        pltpu.VMEM((A_HALO + tt, D_MODEL), F32)],
        compiler_params=_params("arbitrary", "arbitrary"),
        name="mix_shortconv",
    )(x, g, win, cw, wout)


def _layer_norm(y, g, b):
    mu = jnp.mean(y, axis=-1, keepdims=True)
    yc = y - mu
    var = jnp.mean(yc * yc, axis=-1, keepdims=True)
    return yc * lax.rsqrt(var + LN_EPS) * g + b


def _dwconv_tile(ubuf_ref, taps_ref, y_ref, r0):
    sub = V7X_SUBLANES
    nblk = CONV_ROWS // sub
    na = -(-B_WIDTH // sub)
    assert B_HALO == na * sub
    rowid = lax.broadcasted_iota(jnp.int32, (sub, V7X_LANES), 0)
    for lt in range(D_MODEL // V7X_LANES):
        lanes = slice(lt * V7X_LANES, (lt + 1) * V7X_LANES)
        ub = [ubuf_ref[pl.ds(r0 + sub * i, sub), lanes] for i in range(nblk + na)]
        acc = None
        for r in range(sub):
            z = []
            for m in range(nblk + 1):
                zm = None
                for a, s in enumerate(range(r, B_WIDTH, sub)):
                    term = taps_ref[s, :, lanes] * ub[na - 1 + m - a]
                    zm = term if zm is None else zm + term
                z.append(zm)
            if r == 0:
                shifted = z[1:]
            else:
                rolled = [pltpu.roll(zm, r, 0) for zm in z]
                shifted = [jnp.where(rowid >= r, rolled[n + 1], rolled[n]) for n in range(nblk)]
            acc = shifted if acc is None else [p + c for p, c in zip(acc, shifted)]
        for n in range(nblk):
            y_ref[pl.ds(r0 + sub * n, sub), lanes] = acc[n]


def _mixb_kernel(x_ref, g_ref, w1_ref, b1_ref, taps_ref, dwb_ref, lng_ref, lnb_ref, w2_ref, b2_ref,
                 o_ref, st_ref, h_ref, ubuf_ref, y_ref, *, tt):
    x = x_ref[...]
    h_ref[...] = _rms(x, g_ref[...]).astype(BF16)
    _carry_halo(ubuf_ref, B_HALO, tt)
    a = _dot(h_ref[...], w1_ref[:, 0:D_MODEL]) + b1_ref[:, 0:D_MODEL]
    gate = _dot(h_ref[...], w1_ref[:, D_MODEL:2 * D_MODEL]) + b1_ref[:, D_MODEL:2 * D_MODEL]
    ubuf_ref[B_HALO:B_HALO + tt, :] = a * jax.nn.sigmoid(gate)

    def conv_body(i, carry):
        _dwconv_tile(ubuf_ref, taps_ref, y_ref, pl.multiple_of(i * CONV_ROWS, CONV_ROWS))
        return carry

    lax.fori_loop(0, tt // CONV_ROWS, conv_body, 0)
    z = _silu(_layer_norm(y_ref[...] + dwb_ref[...], lng_ref[...], lnb_ref[...]))
    o_ref[...] = x + _dot(z.astype(BF16), w2_ref[...]) + b2_ref[...]

    @pl.when(pl.program_id(1) == pl.num_programs(1) - 1)
    def _():
        st_ref[...] = ubuf_ref[B_HALO + tt - (B_WIDTH - 1):B_HALO + tt, :]


def _mixb_prompt(x, g, w1, b1, taps, dwb, lng, lnb, w2, b2):
    n, t, _ = x.shape
    tt = TOK_TILE
    return pl.pallas_call(
        functools.partial(_mixb_kernel, tt=tt),
        out_shape=(jax.ShapeDtypeStruct((n, t, D_MODEL), F32),
                   jax.ShapeDtypeStruct((n, B_WIDTH - 1, D_MODEL), F32)),
        grid=(n, t // tt),
        in_specs=[pl.BlockSpec((None, tt, D_MODEL), lambda i, j: (i, j, 0)),
                  _const_spec((1, D_MODEL)),
                  _const_spec((D_MODEL, 2 * D_MODEL)),
                  _const_spec((1, 2 * D_MODEL)),
                  _const_spec((B_WIDTH, V7X_SUBLANES, D_MODEL)),
                  _const_spec((1, D_MODEL)),
                  _const_spec((1, D_MODEL)),
                  _const_spec((1, D_MODEL)),
                  _const_spec((D_MODEL, D_MODEL)),
                  _const_spec((1, D_MODEL))],
        out_specs=(pl.BlockSpec((None, tt, D_MODEL), lambda i, j: (i, j, 0)),
                   pl.BlockSpec((None, B_WIDTH - 1, D_MODEL), lambda i, j: (i, 0, 0))),
        scratch_shapes=[pltpu.VMEM((tt, D_MODEL), BF16),
                        pltpu.VMEM((B_HALO + tt, D_MODEL), F32),
                        pltpu.VMEM((tt, D_MODEL), F32)],
        compiler_params=_params("arbitrary", "arbitrary"),
        name="mix_conformer",
    )(x, g, w1, b1, taps, dwb, lng, lnb, w2, b2)


def _hgrn_constants():
    c = HG_CHUNK
    t = np.arange(c)[:, None]
    r = np.arange(c)[None, :]
    blocks = [(r <= t), (r > t)]
    masks = [(t == r)]
    for lvl in range(HG_LEVELS):
        h = 1 << lvl
        m = (t // (2 * h)) * (2 * h) + h - 1
        upper = (t & h) != 0
        blocks.append(np.where(upper, (r > m) & (r <= t), (r > t) & (r <= m)))
        masks.append((((t ^ r) >> lvl) == 1) & upper)
    a_stack = np.concatenate(blocks, axis=0).astype(np.float32)
    a_stack = np.concatenate([a_stack, a_stack], axis=1)
    m_stack = np.stack(masks, axis=0).astype(np.float32)
    return jnp.asarray(a_stack, BF16), jnp.asarray(m_stack, F32)


def _gates(fx, lb):
    sig = jax.nn.sigmoid(fx)
    return lb + (1.0 - lb) * sig, (1.0 - lb) * (1.0 - sig)


def _head_out(o, gx, gn):
    o = o * lax.rsqrt(jnp.mean(o * o, axis=-1, keepdims=True) + NORM_EPS) * gn
    return o * _silu(gx)


def _hgrn_head_chunk(q, kk, v, s, e, m_ref):
    c = HG_CHUNK
    e_b = e(0)
    vb = v.astype(BF16)
    o = _dot((q * e_b).astype(BF16), s.astype(BF16))
    scores = _dot(q.astype(BF16), kk.T.astype(BF16)) * m_ref[0]
    for lvl in range(HG_LEVELS):
        e_l = e(2 + lvl)
        scores = scores + _dot((q * e_l).astype(BF16), (kk * e_l).T.astype(BF16)) * m_ref[1 + lvl]
    o = o + _dot(scores.astype(BF16), vb)
    decay = jnp.broadcast_to(e_b[c - 1:c, :], (C_DK, C_DK)).T
    s_new = s * decay + _dot_tn((kk * e(1)).astype(BF16), vb)
    return o, s_new


def _mixc_kernel(x_ref, g_ref, wq_ref, lb_ref, gn_ref, wout_ref, a_ref, m_ref,
                 o_ref, sout_ref, h_ref, p_ref, qk_ref, e_ref, om_ref, s_ref, *, tt):
    j = pl.program_id(1)
    d = D_MODEL
    c = HG_CHUNK

    @pl.when(j == 0)
    def _():
        s_ref[...] = jnp.zeros_like(s_ref)

    x = x_ref[...]
    h_ref[...] = _rms(x, g_ref[...]).astype(BF16)
    nb = 4 * V7X_LANES
    for b in range(4 * d // nb):
        p_ref[:, b * nb:(b + 1) * nb] = _dot(h_ref[...], wq_ref[:, b * nb:(b + 1) * nb])

    def prep(ci):
        rows = slice(ci * c, (ci + 1) * c)
        f, kk = _gates(p_ref[rows, d:2 * d], lb_ref[...])
        log2f = jnp.log2(f)
        hi = log2f.astype(BF16)
        lo = (log2f - hi.astype(F32)).astype(BF16)
        e_ref[ci % 2] = jnp.exp2(_dot(a_ref[...], jnp.concatenate([hi, lo], axis=0)))
        qk_ref[ci % 2, :, 0:d] = _silu(p_ref[rows, 0:d]) * (C_DK ** -0.5)
        qk_ref[ci % 2, :, d:2 * d] = kk

    def heads(ci):
        rows = slice(ci * c, (ci + 1) * c)
        slot = ci % 2
        for hd in range(C_HEADS):
            lanes = slice(hd * C_DK, (hd + 1) * C_DK)
            o, s_ref[hd] = _hgrn_head_chunk(
                qk_ref[slot, :, lanes], qk_ref[slot, :, d + hd * C_DK:d + (hd + 1) * C_DK],
                p_ref[rows, 2 * d + hd * C_DV:2 * d + (hd + 1) * C_DV], s_ref[hd],
                lambda r: e_ref[slot, r * c:(r + 1) * c, lanes], m_ref)
            gx = p_ref[rows, 3 * d + hd * C_DV:3 * d + (hd + 1) * C_DV]
            om_ref[rows, lanes] = _head_out(o, gx, gn_ref[...]).astype(BF16)

    nchunk = tt // c
    prep(0)
    for ci in range(nchunk):
        if ci + 1 < nchunk:
            prep(ci + 1)
        heads(ci)
    o_ref[...] = x + _dot(om_ref[...], wout_ref[...])

    @pl.when(j == pl.num_programs(1) - 1)
    def _():
        sout_ref[...] = s_ref[...]


def _mixc_prompt(x, g, wq, lb_row, gn, wout, a_stack, m_stack):
    n, t, _ = x.shape
    tt = TOK_TILE
    return pl.pallas_call(
        functools.partial(_mixc_kernel, tt=tt),
        out_shape=(jax.ShapeDtypeStruct((n, t, D_MODEL), F32),
                   jax.ShapeDtypeStruct((n, C_HEADS, C_DK, C_DV), F32)),
        grid=(n, t // tt),
        in_specs=[pl.BlockSpec((None, tt, D_MODEL), lambda i, j: (i, j, 0)),
                  _const_spec((1, D_MODEL)),
                  _const_spec((D_MODEL, 4 * D_MODEL)),
                  _const_spec((1, D_MODEL)),
                  _const_spec((1, C_DV)),
                  _const_spec((D_MODEL, D_MODEL)),
                  _const_spec(a_stack.shape),
                  _const_spec(m_stack.shape)],
        out_specs=(pl.BlockSpec((None, tt, D_MODEL), lambda i, j: (i, j, 0)),
                   pl.BlockSpec((None, C_HEADS, C_DK, C_DV), lambda i, j: (i, 0, 0, 0))),
        scratch_shapes=[pltpu.VMEM((tt, D_MODEL), BF16),
                        pltpu.VMEM((tt, 4 * D_MODEL), F32),
                        pltpu.VMEM((2, HG_CHUNK, 2 * D_MODEL), F32),
                        pltpu.VMEM((2, (2 + HG_LEVELS) * HG_CHUNK, D_MODEL), F32),
                        pltpu.VMEM((tt, D_MODEL), BF16),
                        pltpu.VMEM((C_HEADS, C_DK, C_DV), F32)],
        compiler_params=_params("arbitrary", "arbitrary"),
        name="mix_hgrn2",
    )(x, g, wq, lb_row, gn, wout, a_stack, m_stack)


def _mixa_sample_kernel(x_ref, st_ref, g_ref, win_ref, cw_ref, wout_ref, o_ref, sto_ref):
    x = x_ref[...]
    h = _rms(x, g_ref[...]).astype(BF16)
    p = _dot(h, win_ref[...])
    bg = p[:, 0:D_MODEL]
    u = p[:, D_MODEL:2 * D_MODEL] * p[:, 2 * D_MODEL:3 * D_MODEL]
    y = cw_ref[0:1, :] * st_ref[0] + cw_ref[1:2, :] * st_ref[1] + cw_ref[2:3, :] * u
    o_ref[...] = x + _dot((bg * y).astype(BF16), wout_ref[...])
    sto_ref[0] = st_ref[1]
    sto_ref[1] = u


def _mixa_sample(x2d, st, g, win, cw, wout):
    n = x2d.shape[0]
    return pl.pallas_call(
        _mixa_sample_kernel,
        out_shape=(jax.ShapeDtypeStruct((n, D_MODEL), F32),
                   jax.ShapeDtypeStruct((A_WIDTH - 1, n, D_MODEL), F32)),
        grid=(1,),
        in_specs=[_const_spec((n, D_MODEL)),
                  _const_spec((A_WIDTH - 1, n, D_MODEL)),
                  _const_spec((1, D_MODEL)),
                  _const_spec((D_MODEL, 3 * D_MODEL)),
                  _const_spec((A_WIDTH, D_MODEL)),
                  _const_spec((D_MODEL, D_MODEL))],
        out_specs=(pl.BlockSpec((n, D_MODEL), lambda i: (0, 0)),
                   pl.BlockSpec((A_WIDTH - 1, n, D_MODEL), lambda i: (0, 0, 0))),
        compiler_params=_params("arbitrary"),
        name="mix_shortconv_sample",
    )(x2d, st, g, win, cw, wout)


def _mixb_sample_kernel(x_ref, st_ref, g_ref, w1_ref, b1_ref, dw_ref, dwb_ref, lng_ref, lnb_ref,
                        w2_ref, b2_ref, o_ref, sto_ref, u_ref, y_ref, *, nb):
    i = pl.program_id(0)
    hist = B_WIDTH - 1

    @pl.when(i == 0)
    def _():
        h = _rms(x_ref[...], g_ref[...]).astype(BF16)
        p = _dot(h, w1_ref[...]) + b1_ref[...]
        u_ref[...] = p[:, 0:D_MODEL] * jax.nn.sigmoid(p[:, D_MODEL:2 * D_MODEL])

    rows = pl.ds(pl.multiple_of(i * nb, nb), nb)
    u = u_ref[rows, :]
    y = dw_ref[hist:hist + 1, :] * u
    for k in range(hist):
        y = y + dw_ref[k:k + 1, :] * st_ref[k]
        sto_ref[k] = st_ref[k + 1] if k + 1 < hist else u
    y_ref[rows, :] = y

    @pl.when(i == pl.num_programs(0) - 1)
    def _():
        z = _silu(_layer_norm(y_ref[...] + dwb_ref[...], lng_ref[...], lnb_ref[...]))
        o_ref[...] = x_ref[...] + _dot(z.astype(BF16), w2_ref[...]) + b2_ref[...]


def _mixb_sample(x2d, st, layer, g, w1, b1, dw, dwb, lng, lnb, w2, b2):
    n = x2d.shape[0]
    nb = SAMPLE_BLOCK_B
    hist = B_WIDTH - 1
    return pl.pallas_call(
        functools.partial(_mixb_sample_kernel, nb=nb),
        out_shape=(jax.ShapeDtypeStruct((n, D_MODEL), F32),
                   jax.ShapeDtypeStruct((hist, n, D_MODEL), F32)),
        grid=(n // nb,),
        in_specs=[_const_spec((n, D_MODEL)),
                  pl.BlockSpec((None, hist, nb, D_MODEL), lambda i: (layer, 0, i, 0)),
                  _const_spec((1, D_MODEL)),
                  _const_spec((D_MODEL, 2 * D_MODEL)),
                  _const_spec((1, 2 * D_MODEL)),
                  _const_spec((B_WIDTH, D_MODEL)),
                  _const_spec((1, D_MODEL)),
                  _const_spec((1, D_MODEL)),
                  _const_spec((1, D_MODEL)),
                  _const_spec((D_MODEL, D_MODEL)),
                  _const_spec((1, D_MODEL))],
        out_specs=(pl.BlockSpec((n, D_MODEL), lambda i: (0, 0)),
                   pl.BlockSpec((hist, nb, D_MODEL), lambda i: (0, i, 0))),
        scratch_shapes=[pltpu.VMEM((n, D_MODEL), F32),
                        pltpu.VMEM((n, D_MODEL), F32)],
        compiler_params=_params("arbitrary"),
        name="mix_conformer_sample",
    )(x2d, st, g, w1, b1, dw, dwb, lng, lnb, w2, b2)


def _sample_selector():
    nb = SAMPLE_BLOCK_C
    sel = np.zeros((4 * nb, nb * C_DV), np.float32)
    for p in range(3):
        for r in range(nb):
            sel[p * nb + r, r * C_DV:(r + 1) * C_DV] = 1.0
    return jnp.asarray(sel, BF16)


def _split3_rows(a):
    hi = a.astype(BF16).astype(F32)
    r1 = a - hi
    mid = r1.astype(BF16).astype(F32)
    lo = r1 - mid
    return jnp.concatenate([hi, mid, lo, jnp.zeros_like(a)], axis=0).astype(BF16)


def _mixc_sample_kernel(x_ref, s_ref, g_ref, wq_ref, lb_ref, gn_ref, wout_ref, sel_ref,
                        o_ref, so_ref, p_ref, orow_ref, om_ref, *, nb):
    i = pl.program_id(0)

    @pl.when(i == 0)
    def _():
        h = _rms(x_ref[...], g_ref[...]).astype(BF16)
        p_ref[...] = _dot(h, wq_ref[...])

    rows = pl.ds(pl.multiple_of(i * nb, nb), nb)
    d = D_MODEL
    f, kk = _gates(p_ref[rows, d:2 * d], lb_ref[...])
    q = _silu(p_ref[rows, 0:d]) * (C_DK ** -0.5)
    v = p_ref[rows, 2 * d:3 * d]

    for hd in range(C_HEADS):
        lanes = slice(hd * C_DK, (hd + 1) * C_DK)
        parts = jnp.concatenate([_split3_rows(a[:, lanes]) for a in (f, kk, q)], axis=1)
        cols = _dot_tn(parts, sel_ref[...])
        for r in range(nb):
            blk = slice(r * C_DV, (r + 1) * C_DV)
            s_new = cols[0:C_DK, blk] * s_ref[r, hd] + cols[C_DK:2 * C_DK, blk] * v[r:r + 1, lanes]
            so_ref[r, hd] = s_new
            orow_ref[r:r + 1, lanes] = jnp.sum(cols[2 * C_DK:3 * C_DK, blk] * s_new, axis=0,
                                               keepdims=True)

    gx = p_ref[rows, 3 * d:4 * d]
    for hd in range(C_HEADS):
        lanes = slice(hd * C_DK, (hd + 1) * C_DK)
        om_ref[rows, lanes] = _head_out(orow_ref[:, lanes], gx[:, lanes], gn_ref[...])

    @pl.when(i == pl.num_programs(0) - 1)
    def _():
        o_ref[...] = x_ref[...] + _dot(om_ref[...].astype(BF16), wout_ref[...])


def _mixc_sample(x2d, s, g, wq, lb_row, gn, wout, sel):
    n = x2d.shape[0]
    nb = SAMPLE_BLOCK_C
    return pl.pallas_call(
        functools.partial(_mixc_sample_kernel, nb=nb),
        out_shape=(jax.ShapeDtypeStruct((n, D_MODEL), F32),
                   jax.ShapeDtypeStruct((n, C_HEADS, C_DK, C_DV), F32)),
        grid=(n // nb,),
        in_specs=[_const_spec((n, D_MODEL)),
                  pl.BlockSpec((nb, C_HEADS, C_DK, C_DV), lambda i: (i, 0, 0, 0)),
                  _const_spec((1, D_MODEL)),
                  _const_spec((D_MODEL, 4 * D_MODEL)),
                  _const_spec((1, D_MODEL)),
                  _const_spec((1, C_DV)),
                  _const_spec((D_MODEL, D_MODEL)),
                  _const_spec(sel.shape)],
        out_specs=(pl.BlockSpec((n, D_MODEL), lambda i: (0, 0)),
                   pl.BlockSpec((nb, C_HEADS, C_DK, C_DV), lambda i: (i, 0, 0, 0))),
        scratch_shapes=[pltpu.VMEM((n, 4 * D_MODEL), F32),
                        pltpu.VMEM((nb, D_MODEL), F32),
                        pltpu.VMEM((n, D_MODEL), F32)],
        compiler_params=_params("arbitrary"),
        name="mix_hgrn2_sample",
    )(x2d, s, g, wq, lb_row, gn, wout, sel)


def _row(a):
    return a.reshape(1, -1)


def kernel(x_prompt, x_sample, state_conva, state_convb, state_hgrn, norm_mix, a_w_in, a_conv_w, a_w_out, b_w_pw1, b_b_pw1, b_dw_w, b_dw_b, b_ln_g, b_ln_b, b_w_pw2, b_b_pw2, c_lower_bounds, c_w_qfig, c_gnorm, c_w_out, norm_ffn, ffn_w_gate_up, ffn_w_down, norm_final):
    nb, t, d = x_prompt.shape
    ns = x_sample.shape[0]
    bf = lambda w: w.astype(BF16)
    a_w_in, a_w_out, b_w_pw1, b_w_pw2 = bf(a_w_in), bf(a_w_out), bf(b_w_pw1), bf(b_w_pw2)
    c_w_qfig, c_w_out = bf(c_w_qfig), bf(c_w_out)

    sm = jax.nn.softmax(c_lower_bounds.astype(F32), axis=0)
    lower = jnp.cumsum(sm, axis=0) - sm[0]
    a_stack, m_stack = _hgrn_constants()
    gfin = _row(norm_final)

    xp = x_prompt
    xs = x_sample.reshape(ns, d)
    g_ffn = norm_ffn.reshape(DEPTH, 1, d)
    conva_p, conva_s, convb_p, convb_s, hgrn_p, hgrn_s = [], [], [], [], [], []
    for i in range(DEPTH):
        kind, j = i % 3, i // 3
        g = _row(norm_mix[i])
        if kind == 0:
            w = (g, a_w_in[j], a_conv_w[j], a_w_out[j])
            xp, st = _mixa_prompt(xp, *w)
            conva_p.append(st)
            xs, st = _mixa_sample(xs, jnp.swapaxes(state_conva[j], 0, 1), *w)
            conva_s.append(jnp.swapaxes(st, 0, 1))
        elif kind == 1:
            w = (b_w_pw1[j], _row(b_b_pw1[j]))
            w2 = (_row(b_dw_b[j]), _row(b_ln_g[j]), _row(b_ln_b[j]), b_w_pw2[j], _row(b_b_pw2[j]))
            taps = jnp.broadcast_to(b_dw_w[j][::-1][:, None, :], (B_WIDTH, V7X_SUBLANES, d))
            xp, st = _mixb_prompt(xp, g, *w, taps, *w2)
            convb_p.append(st)
            xs, st = _mixb_sample(xs, jnp.swapaxes(state_convb, 1, 2), j, g, *w, b_dw_w[j], *w2)
            convb_s.append(jnp.swapaxes(st, 0, 1))
        else:
            gn = _row(c_gnorm[j])
            xp, st = _mixc_prompt(xp, g, c_w_qfig[j], _row(lower[i]), gn, c_w_out[j], a_stack, m_stack)
            hgrn_p.append(st)
            xs, st = _mixc_sample(xs, state_hgrn[j], g, c_w_qfig[j], _row(lower[i]), gn, c_w_out[j],
                                  _sample_selector())
            hgrn_s.append(st)
        xp, xs = _ffn(xp.reshape(nb * t, d), xs, g_ffn, ffn_w_gate_up, ffn_w_down, gfin, layer=i,
                      final=i == DEPTH - 1)
        xp = xp.reshape(nb, t, d)
    stack = lambda xs_: xs_[0][None] if len(xs_) == 1 else jnp.stack(xs_)
    return (xp, xs.reshape(ns, 1, d), stack(conva_p), stack(conva_s), stack(convb_p),
            stack(convb_s), stack(hgrn_p), stack(hgrn_s))
```

```python
import functools

import numpy as np
import jax
import jax.numpy as jnp
from jax import lax
from jax.experimental import pallas as pl
from jax.experimental.pallas import tpu as pltpu

F32 = jnp.float32
BF16 = jnp.bfloat16

D_MODEL = 1024
DEPTH = 4
D_FF = 2816
A_WIDTH = 3
B_WIDTH = 31
C_HEADS = 8
C_DK = 128
C_DV = 128
NORM_EPS = 1e-6
LN_EPS = 1e-5

V7X_SUBLANES = 8
V7X_LANES = 128
V7X_VMEM_LIMIT_BYTES = 60 * 1024 * 1024

TOK_TILE = 512
FF_CHUNK = 256
FFN_PREP_STEPS = 16
HG_CHUNK = 128
HG_LEVELS = 7
SAMPLE_BLOCK_B = 16
SAMPLE_BLOCK_C = 8
A_HALO = V7X_SUBLANES
B_HALO = 32
CONV_ROWS = 64


def _params(*sem):
    return pltpu.CompilerParams(dimension_semantics=sem,
                                vmem_limit_bytes=V7X_VMEM_LIMIT_BYTES)


def _const_spec(shape):
    nd = len(shape)
    return pl.BlockSpec(shape, lambda *_: (0,) * nd, pipeline_mode=pl.Buffered(1))


def _layer_spec(shape, layer):
    nd = len(shape)
    return pl.BlockSpec((None,) + tuple(shape), lambda *_: (layer,) + (0,) * nd,
                        pipeline_mode=pl.Buffered(1))


def _rms(x, g):
    ms = jnp.mean(x * x, axis=-1, keepdims=True)
    return x * lax.rsqrt(ms + NORM_EPS) * g


def _silu(x):
    return x * jax.nn.sigmoid(x)


def _dot(a, b):
    return jnp.dot(a, b, preferred_element_type=F32)


def _dot_nt(a, b):
    return lax.dot_general(a, b, (((1,), (1,)), ((), ())), preferred_element_type=F32)


def _dot_tn(a, b):
    return lax.dot_general(a, b, (((0,), (0,)), ((), ())), preferred_element_type=F32)


def _ffn_rows(x_ref, g_ref, wgu_ref, wd_ref, gf_ref, h_ref, a_ref, final, h_ready=False,
              after_chunk=None):
    if not h_ready:
        h_ref[...] = _rms(x_ref[...], g_ref[...]).astype(BF16)
    for j in range(D_FF // FF_CHUNK):
        lo = j * FF_CHUNK
        gate = _dot(h_ref[...], wgu_ref[:, lo:lo + FF_CHUNK])
        up = _dot(h_ref[...], wgu_ref[:, D_FF + lo:D_FF + lo + FF_CHUNK])
        a_ref[:, lo:lo + FF_CHUNK] = (_silu(gate) * up).astype(BF16)
        if after_chunk is not None:
            after_chunk(j, gate)
    y = x_ref[...] + _dot(a_ref[...], wd_ref[...])
    return _rms(y, gf_ref[...]) if final else y


def _ffn_kernel(x_ref, xs_ref, g_ref, wgu_hbm_ref, wd_hbm_ref, gf_ref, o_ref, os_ref,
                wgu_ref, wd_ref, h_ref, a_ref, *, final):
    i = pl.program_id(0)
    args = (g_ref, wgu_ref, wd_ref, gf_ref)

    @pl.when(i < FFN_PREP_STEPS)
    def _():
        ru, rd = wgu_hbm_ref.shape[0], wd_hbm_ref.shape[0]
        wgu_ref[pl.ds(pl.multiple_of(i * ru, ru), ru), :] = wgu_hbm_ref[...].astype(BF16)
        wd_ref[pl.ds(pl.multiple_of(i * rd, rd), rd), :] = wd_hbm_ref[...].astype(BF16)

    @pl.when(i >= FFN_PREP_STEPS)
    def _():
        o_ref[...] = _ffn_rows(x_ref, *args, h_ref, a_ref, final)

    @pl.when(i == pl.num_programs(0) - 1)
    def _():
        ns = xs_ref.shape[0]
        os_ref[...] = _ffn_rows(xs_ref, *args, h_ref.at[0:ns], a_ref.at[0:ns], final)


def _ffn(x2d, xs, g3, wgu, wd, gf, *, layer, final):
    m = x2d.shape[0]
    ns = xs.shape[0]
    tm = TOK_TILE
    npre = FFN_PREP_STEPS
    ru, rd = D_MODEL // npre, D_FF // npre
    tile = lambda i: (jnp.maximum(i - npre, 0), 0)
    slab = lambda i: (layer, jnp.minimum(i, npre - 1), 0)
    return pl.pallas_call(
        functools.partial(_ffn_kernel, final=final),
        out_shape=(jax.ShapeDtypeStruct((m, D_MODEL), F32),
                   jax.ShapeDtypeStruct((ns, D_MODEL), F32)),
        grid=(npre + m // tm,),
        in_specs=[pl.BlockSpec((tm, D_MODEL), tile),
                  _const_spec((ns, D_MODEL)),
                  _layer_spec((1, D_MODEL), layer),
                  pl.BlockSpec((None, ru, 2 * D_FF), slab),
                  pl.BlockSpec((None, rd, D_MODEL), slab),
                  _const_spec((1, D_MODEL))],
        out_specs=(pl.BlockSpec((tm, D_MODEL), tile),
                   pl.BlockSpec((ns, D_MODEL), lambda i: (0, 0))),
        scratch_shapes=[pltpu.VMEM((D_MODEL, 2 * D_FF), BF16),
                        pltpu.VMEM((D_FF, D_MODEL), BF16),
                        pltpu.VMEM((tm, D_MODEL), BF16),
                        pltpu.VMEM((tm, D_FF), BF16)],
        compiler_params=_params("arbitrary"),
        name="ffn",
    )(x2d, xs, g3, wgu, wd, gf)


def _carry_halo(ubuf_ref, halo, tt):
    j = pl.program_id(1)

    @pl.when(j == 0)
    def _():
        ubuf_ref[0:halo, :] = jnp.zeros((halo, D_MODEL), F32)

    @pl.when(j > 0)
    def _():
        ubuf_ref[0:halo, :] = ubuf_ref[tt:tt + halo, :]


def _mixa_kernel(x_ref, g_ref, win_ref, cw_ref, wout_ref, o_ref, st_ref, h_ref, ubuf_ref, *, tt):
    x = x_ref[...]
    h_ref[...] = _rms(x, g_ref[...]).astype(BF16)
    _carry_halo(ubuf_ref, A_HALO, tt)
    cg = _dot(h_ref[...], win_ref[:, D_MODEL:2 * D_MODEL])
    hv = _dot(h_ref[...], win_ref[:, 2 * D_MODEL:3 * D_MODEL])
    ubuf_ref[A_HALO:A_HALO + tt, :] = cg * hv
    y = cw_ref[A_WIDTH - 1:A_WIDTH, :] * ubuf_ref[A_HALO:A_HALO + tt, :]
    for k in range(A_WIDTH - 1):
        off = A_HALO - (A_WIDTH - 1) + k
        y = y + cw_ref[k:k + 1, :] * ubuf_ref[off:off + tt, :]
    bg = _dot(h_ref[...], win_ref[:, 0:D_MODEL])
    o_ref[...] = x + _dot((bg * y).astype(BF16), wout_ref[...])

    @pl.when(pl.program_id(1) == pl.num_programs(1) - 1)
    def _():
        st_ref[...] = ubuf_ref[A_HALO + tt - (A_WIDTH - 1):A_HALO + tt, :]


def _mixa_prompt(x, g, win, cw, wout):
    n, t, _ = x.shape
    tt = TOK_TILE
    return pl.pallas_call(
        functools.partial(_mixa_kernel, tt=tt),
        out_shape=(jax.ShapeDtypeStruct((n, t, D_MODEL), F32),
                   jax.ShapeDtypeStruct((n, A_WIDTH - 1, D_MODEL), F32)),
        grid=(n, t // tt),
        in_specs=[pl.BlockSpec((None, tt, D_MODEL), lambda i, j: (i, j, 0)),
                  _const_spec((1, D_MODEL)),
                  _const_spec((D_MODEL, 3 * D_MODEL)),
                  _const_spec((A_WIDTH, D_MODEL)),
                  _const_spec((D_MODEL, D_MODEL))],
        out_specs=(pl.BlockSpec((None, tt, D_MODEL), lambda i, j: (i, j, 0)),
                   pl.BlockSpec((None, A_WIDTH - 1, D_MODEL), lambda i, j: (i, 0, 0))),
        scratch_shapes=[pltpu.VMEM((tt, D_MODEL), BF16),
                        pltpu.VMEM((A_HALO + tt, D_MODEL), F32)],
        compiler_params=_params("arbitrary", "arbitrary"),
        name="mix_shortconv",
    )(x, g, win, cw, wout)


def _layer_norm(y, g, b):
    mu = jnp.mean(y, axis=-1, keepdims=True)
    yc = y - mu
    var = jnp.mean(yc * yc, axis=-1, keepdims=True)
    return yc * lax.rsqrt(var + LN_EPS) * g + b


def _dwconv_tile(ubuf_ref, taps_ref, y_ref, r0, anchor=None):
    sub = V7X_SUBLANES
    nblk = CONV_ROWS // sub
    na = -(-B_WIDTH // sub)
    assert B_HALO == na * sub
    rowid = lax.broadcasted_iota(jnp.int32, (sub, V7X_LANES), 0)
    for lt in range(D_MODEL // V7X_LANES):
        lanes = slice(lt * V7X_LANES, (lt + 1) * V7X_LANES)
        ub = [ubuf_ref[pl.ds(r0 + sub * i, sub), lanes] for i in range(nblk + na)]
        if anchor is not None:
            ub[0] = ub[0] + 0.0 * anchor
        acc = None
        for r in range(sub):
            z = []
            for m in range(nblk + 1):
                zm = None
                for a, s in enumerate(range(r, B_WIDTH, sub)):
                    term = taps_ref[s, :, lanes] * ub[na - 1 + m - a]
                    zm = term if zm is None else zm + term
                z.append(zm)
            if r == 0:
                shifted = z[1:]
            else:
                rolled = [pltpu.roll(zm, r, 0) for zm in z]
                shifted = [jnp.where(rowid >= r, rolled[n + 1], rolled[n]) for n in range(nblk)]
            acc = shifted if acc is None else [p + c for p, c in zip(acc, shifted)]
        for n in range(nblk):
            y_ref[pl.ds(r0 + sub * n, sub), lanes] = acc[n]


def _mixb_ffn_kernel(x_ref, xs_ref, g_ref, w1_ref, b1_ref, taps_ref, dwb_ref, lng_ref, lnb_ref, w2_ref,
                     b2_ref, gffn_ref, wgu_hbm_ref, wd_hbm_ref, gf_ref, o_ref, os_ref, st_ref,
                     wgu_ref, wd_ref, hb_ref, ubuf_ref, y_ref, mid_ref, hf_ref, a_ref,
                     *, tt, tiles_per_seq, ntiles, final):
    i = pl.program_id(0)
    t = jnp.minimum(i - FFN_PREP_STEPS, ntiles - 1)
    ffn_args = (gffn_ref, wgu_ref, wd_ref, gf_ref)

    @pl.when(i < FFN_PREP_STEPS)
    def _():
        ru, rd = wgu_hbm_ref.shape[0], wd_hbm_ref.shape[0]
        wgu_ref[pl.ds(pl.multiple_of(i * ru, ru), ru), :] = wgu_hbm_ref[...].astype(BF16)
        wd_ref[pl.ds(pl.multiple_of(i * rd, rd), rd), :] = wd_hbm_ref[...].astype(BF16)

    @pl.when(i == FFN_PREP_STEPS)
    def _():
        mid_ref[...] = jnp.zeros_like(mid_ref)
        hf_ref[...] = jnp.zeros_like(hf_ref)
        ubuf_ref[...] = jnp.zeros_like(ubuf_ref)

    @pl.when(i >= FFN_PREP_STEPS)
    def _():
        hb_ref[...] = _rms(x_ref[...], g_ref[...]).astype(BF16)
        ubuf_ref[0:B_HALO, :] = jnp.where(t % tiles_per_seq > 0, ubuf_ref[tt:tt + B_HALO, :], 0.0)
        a = _dot(hb_ref[...], w1_ref[:, 0:D_MODEL]) + b1_ref[:, 0:D_MODEL]
        gate = _dot(hb_ref[...], w1_ref[:, D_MODEL:2 * D_MODEL]) + b1_ref[:, D_MODEL:2 * D_MODEL]
        ubuf_ref[B_HALO:B_HALO + tt, :] = a * jax.nn.sigmoid(gate)

        def conv_block(j, gate):
            if j < tt // CONV_ROWS:
                _dwconv_tile(ubuf_ref, taps_ref, y_ref, j * CONV_ROWS,
                             anchor=gate[0:V7X_SUBLANES, 0:V7X_LANES])

        assert tt // CONV_ROWS <= D_FF // FF_CHUNK
        o_ref[...] = _ffn_rows(mid_ref, *ffn_args, hf_ref, a_ref, final, h_ready=True,
                               after_chunk=conv_block)

        z = _silu(_layer_norm(y_ref[...] + dwb_ref[...], lng_ref[...], lnb_ref[...]))
        mid_ref[...] = x_ref[...] + _dot(z.astype(BF16), w2_ref[...]) + b2_ref[...]
        hf_ref[...] = _rms(mid_ref[...], gffn_ref[...]).astype(BF16)
        st_ref[...] = ubuf_ref[B_HALO + tt - (B_WIDTH - 1):B_HALO + tt, :]

    @pl.when(i == pl.num_programs(0) - 1)
    def _():
        ns = xs_ref.shape[0]
        os_ref[...] = _ffn_rows(xs_ref, *ffn_args, hf_ref.at[0:ns], a_ref.at[0:ns], final)


def _mixb_ffn(x, xs, g, w1, b1, taps, dwb, lng, lnb, w2, b2, g3, wgu, wd, gf, *, layer, final):
    n, t, _ = x.shape
    ns = xs.shape[0]
    tt = TOK_TILE
    tps = t // tt
    ntiles = n * tps
    npre = FFN_PREP_STEPS
    ru, rd = D_MODEL // npre, D_FF // npre
    slab = lambda i: (layer, jnp.minimum(i, npre - 1), 0)

    def mixed(i):
        tc = jnp.clip(i - npre, 0, ntiles - 1)
        return tc // tps, tc % tps

    def fed(i):
        tc = jnp.clip(i - npre - 1, 0, ntiles - 1)
        return tc // tps, tc % tps

    return pl.pallas_call(
        functools.partial(_mixb_ffn_kernel, tt=tt, tiles_per_seq=tps, ntiles=ntiles, final=final),
        out_shape=(jax.ShapeDtypeStruct((n, t, D_MODEL), F32),
                   jax.ShapeDtypeStruct((ns, D_MODEL), F32),
                   jax.ShapeDtypeStruct((n, B_WIDTH - 1, D_MODEL), F32)),
        grid=(npre + ntiles + 1,),
        in_specs=[pl.BlockSpec((None, tt, D_MODEL), lambda i: (*mixed(i), 0)),
                  _const_spec((ns, D_MODEL)),
                  _const_spec((1, D_MODEL)),
                  _const_spec((D_MODEL, 2 * D_MODEL)),
                  _const_spec((1, 2 * D_MODEL)),
                  _const_spec((B_WIDTH, V7X_SUBLANES, D_MODEL)),
                  _const_spec((1, D_MODEL)),
                  _const_spec((1, D_MODEL)),
                  _const_spec((1, D_MODEL)),
                  _const_spec((D_MODEL, D_MODEL)),
                  _const_spec((1, D_MODEL)),
                  _layer_spec((1, D_MODEL), layer),
                  pl.BlockSpec((None, ru, 2 * D_FF), slab),
                  pl.BlockSpec((None, rd, D_MODEL), slab),
                  _const_spec((1, D_MODEL))],
        out_specs=(pl.BlockSpec((None, tt, D_MODEL), lambda i: (*fed(i), 0)),
                   pl.BlockSpec((ns, D_MODEL), lambda i: (0, 0)),
                   pl.BlockSpec((None, B_WIDTH - 1, D_MODEL), lambda i: (mixed(i)[0], 0, 0))),
        scratch_shapes=[pltpu.VMEM((D_MODEL, 2 * D_FF), BF16),
                        pltpu.VMEM((D_FF, D_MODEL), BF16),
                        pltpu.VMEM((tt, D_MODEL), BF16),
                        pltpu.VMEM((B_HALO + tt, D_MODEL), F32),
                        pltpu.VMEM((tt, D_MODEL), F32),
                        pltpu.VMEM((tt, D_MODEL), F32),
                        pltpu.VMEM((tt, D_MODEL), BF16),
                        pltpu.VMEM((tt, D_FF), BF16)],
        compiler_params=_params("arbitrary"),
        name="mix_conformer_ffn",
    )(x, xs, g, w1, b1, taps, dwb, lng, lnb, w2, b2, g3, wgu, wd, gf)


def _hgrn_constants():
    c = HG_CHUNK
    t = np.arange(c)[:, None]
    r = np.arange(c)[None, :]
    blocks = [(r <= t), (r > t)]
    masks = [(t == r)]
    for lvl in range(HG_LEVELS):
        h = 1 << lvl
        m = (t // (2 * h)) * (2 * h) + h - 1
        upper = (t & h) != 0
        blocks.append(np.where(upper, (r > m) & (r <= t), (r > t) & (r <= m)))
        masks.append((((t ^ r) >> lvl) == 1) & upper)
    a_stack = np.concatenate(blocks, axis=0).astype(np.float32)
    a_stack = np.concatenate([a_stack, a_stack], axis=1)
    m_stack = np.stack(masks, axis=0).astype(np.float32)
    return jnp.asarray(a_stack, BF16), jnp.asarray(m_stack, F32)


def _gates(fx, lb):
    sig = jax.nn.sigmoid(fx)
    return lb + (1.0 - lb) * sig, (1.0 - lb) * (1.0 - sig)


def _head_out(o, gx, gn):
    o = o * lax.rsqrt(jnp.mean(o * o, axis=-1, keepdims=True) + NORM_EPS) * gn
    return o * _silu(gx)


def _hgrn_head_chunk(q, kk, v, s, e, m_ref):
    c = HG_CHUNK
    e_b = e(0)
    vb = v.astype(BF16)
    o = _dot((q * e_b).astype(BF16), s.astype(BF16))
    scores = _dot(q.astype(BF16), kk.T.astype(BF16)) * m_ref[0]
    for lvl in range(HG_LEVELS):
        e_l = e(2 + lvl)
        scores = scores + _dot((q * e_l).astype(BF16), (kk * e_l).T.astype(BF16)) * m_ref[1 + lvl]
    o = o + _dot(scores.astype(BF16), vb)
    decay = jnp.broadcast_to(e_b[c - 1:c, :], (C_DK, C_DK)).T
    s_new = s * decay + _dot_tn((kk * e(1)).astype(BF16), vb)
    return o, s_new


def _mixc_kernel(x_ref, g_ref, wq_ref, lb_ref, gn_ref, wout_ref, a_ref, m_ref,
                 o_ref, sout_ref, h_ref, p_ref, qk_ref, e_ref, om_ref, s_ref, *, tt):
    j = pl.program_id(1)
    d = D_MODEL
    c = HG_CHUNK

    @pl.when(j == 0)
    def _():
        s_ref[...] = jnp.zeros_like(s_ref)

    x = x_ref[...]
    h_ref[...] = _rms(x, g_ref[...]).astype(BF16)
    nb = 4 * V7X_LANES
    for b in range(4 * d // nb):
        p_ref[:, b * nb:(b + 1) * nb] = _dot(h_ref[...], wq_ref[:, b * nb:(b + 1) * nb])

    def prep(ci):
        rows = slice(ci * c, (ci + 1) * c)
        f, kk = _gates(p_ref[rows, d:2 * d], lb_ref[...])
        log2f = jnp.log2(f)
        hi = log2f.astype(BF16)
        lo = (log2f - hi.astype(F32)).astype(BF16)
        e_ref[ci % 2] = jnp.exp2(_dot(a_ref[...], jnp.concatenate([hi, lo], axis=0)))
        qk_ref[ci % 2, :, 0:d] = _silu(p_ref[rows, 0:d]) * (C_DK ** -0.5)
        qk_ref[ci % 2, :, d:2 * d] = kk

    def heads(ci):
        rows = slice(ci * c, (ci + 1) * c)
        slot = ci % 2
        for hd in range(C_HEADS):
            lanes = slice(hd * C_DK, (hd + 1) * C_DK)
            o, s_ref[hd] = _hgrn_head_chunk(
                qk_ref[slot, :, lanes], qk_ref[slot, :, d + hd * C_DK:d + (hd + 1) * C_DK],
                p_ref[rows, 2 * d + hd * C_DV:2 * d + (hd + 1) * C_DV], s_ref[hd],
                lambda r: e_ref[slot, r * c:(r + 1) * c, lanes], m_ref)
            gx = p_ref[rows, 3 * d + hd * C_DV:3 * d + (hd + 1) * C_DV]
            om_ref[rows, lanes] = _head_out(o, gx, gn_ref[...]).astype(BF16)

    nchunk = tt // c
    prep(0)
    for ci in range(nchunk):
        if ci + 1 < nchunk:
            prep(ci + 1)
        heads(ci)
    o_ref[...] = x + _dot(om_ref[...], wout_ref[...])

    @pl.when(j == pl.num_programs(1) - 1)
    def _():
        sout_ref[...] = s_ref[...]


def _mixc_prompt(x, g, wq, lb_row, gn, wout, a_stack, m_stack):
    n, t, _ = x.shape
    tt = TOK_TILE
    return pl.pallas_call(
        functools.partial(_mixc_kernel, tt=tt),
        out_shape=(jax.ShapeDtypeStruct((n, t, D_MODEL), F32),
                   jax.ShapeDtypeStruct((n, C_HEADS, C_DK, C_DV), F32)),
        grid=(n, t // tt),
        in_specs=[pl.BlockSpec((None, tt, D_MODEL), lambda i, j: (i, j, 0)),
                  _const_spec((1, D_MODEL)),
                  _const_spec((D_MODEL, 4 * D_MODEL)),
                  _const_spec((1, D_MODEL)),
                  _const_spec((1, C_DV)),
                  _const_spec((D_MODEL, D_MODEL)),
                  _const_spec(a_stack.shape),
                  _const_spec(m_stack.shape)],
        out_specs=(pl.BlockSpec((None, tt, D_MODEL), lambda i, j: (i, j, 0)),
                   pl.BlockSpec((None, C_HEADS, C_DK, C_DV), lambda i, j: (i, 0, 0, 0))),
        scratch_shapes=[pltpu.VMEM((tt, D_MODEL), BF16),
                        pltpu.VMEM((tt, 4 * D_MODEL), F32),
                        pltpu.VMEM((2, HG_CHUNK, 2 * D_MODEL), F32),
                        pltpu.VMEM((2, (2 + HG_LEVELS) * HG_CHUNK, D_MODEL), F32),
                        pltpu.VMEM((tt, D_MODEL), BF16),
                        pltpu.VMEM((C_HEADS, C_DK, C_DV), F32)],
        compiler_params=_params("arbitrary", "arbitrary"),
        name="mix_hgrn2",
    )(x, g, wq, lb_row, gn, wout, a_stack, m_stack)


def _mixa_sample_kernel(x_ref, st_ref, g_ref, win_ref, cw_ref, wout_ref, o_ref, sto_ref):
    x = x_ref[...]
    h = _rms(x, g_ref[...]).astype(BF16)
    p = _dot(h, win_ref[...])
    bg = p[:, 0:D_MODEL]
    u = p[:, D_MODEL:2 * D_MODEL] * p[:, 2 * D_MODEL:3 * D_MODEL]
    y = cw_ref[0:1, :] * st_ref[0] + cw_ref[1:2, :] * st_ref[1] + cw_ref[2:3, :] * u
    o_ref[...] = x + _dot((bg * y).astype(BF16), wout_ref[...])
    sto_ref[0] = st_ref[1]
    sto_ref[1] = u


def _mixa_sample(x2d, st, g, win, cw, wout):
    n = x2d.shape[0]
    return pl.pallas_call(
        _mixa_sample_kernel,
        out_shape=(jax.ShapeDtypeStruct((n, D_MODEL), F32),
                   jax.ShapeDtypeStruct((A_WIDTH - 1, n, D_MODEL), F32)),
        grid=(1,),
        in_specs=[_const_spec((n, D_MODEL)),
                  _const_spec((A_WIDTH - 1, n, D_MODEL)),
                  _const_spec((1, D_MODEL)),
                  _const_spec((D_MODEL, 3 * D_MODEL)),
                  _const_spec((A_WIDTH, D_MODEL)),
                  _const_spec((D_MODEL, D_MODEL))],
        out_specs=(pl.BlockSpec((n, D_MODEL), lambda i: (0, 0)),
                   pl.BlockSpec((A_WIDTH - 1, n, D_MODEL), lambda i: (0, 0, 0))),
        compiler_params=_params("arbitrary"),
        name="mix_shortconv_sample",
    )(x2d, st, g, win, cw, wout)


def _mixb_sample_kernel(x_ref, st_ref, g_ref, w1_ref, b1_ref, dw_ref, dwb_ref, lng_ref, lnb_ref,
                        w2_ref, b2_ref, o_ref, sto_ref, u_ref, y_ref, *, nb):
    i = pl.program_id(0)
    hist = B_WIDTH - 1

    @pl.when(i == 0)
    def _():
        h = _rms(x_ref[...], g_ref[...]).astype(BF16)
        p = _dot(h, w1_ref[...]) + b1_ref[...]
        u_ref[...] = p[:, 0:D_MODEL] * jax.nn.sigmoid(p[:, D_MODEL:2 * D_MODEL])

    rows = pl.ds(pl.multiple_of(i * nb, nb), nb)
    u = u_ref[rows, :]
    y = dw_ref[hist:hist + 1, :] * u
    for k in range(hist):
        y = y + dw_ref[k:k + 1, :] * st_ref[k]
        sto_ref[k] = st_ref[k + 1] if k + 1 < hist else u
    y_ref[rows, :] = y

    @pl.when(i == pl.num_programs(0) - 1)
    def _():
        z = _silu(_layer_norm(y_ref[...] + dwb_ref[...], lng_ref[...], lnb_ref[...]))
        o_ref[...] = x_ref[...] + _dot(z.astype(BF16), w2_ref[...]) + b2_ref[...]


def _mixb_sample(x2d, st, layer, g, w1, b1, dw, dwb, lng, lnb, w2, b2):
    n = x2d.shape[0]
    nb = SAMPLE_BLOCK_B
    hist = B_WIDTH - 1
    return pl.pallas_call(
        functools.partial(_mixb_sample_kernel, nb=nb),
        out_shape=(jax.ShapeDtypeStruct((n, D_MODEL), F32),
                   jax.ShapeDtypeStruct((hist, n, D_MODEL), F32)),
        grid=(n // nb,),
        in_specs=[_const_spec((n, D_MODEL)),
                  pl.BlockSpec((None, hist, nb, D_MODEL), lambda i: (layer, 0, i, 0)),
                  _const_spec((1, D_MODEL)),
                  _const_spec((D_MODEL, 2 * D_MODEL)),
                  _const_spec((1, 2 * D_MODEL)),
                  _const_spec((B_WIDTH, D_MODEL)),
                  _const_spec((1, D_MODEL)),
                  _const_spec((1, D_MODEL)),
                  _const_spec((1, D_MODEL)),
                  _const_spec((D_MODEL, D_MODEL)),
                  _const_spec((1, D_MODEL))],
        out_specs=(pl.BlockSpec((n, D_MODEL), lambda i: (0, 0)),
                   pl.BlockSpec((hist, nb, D_MODEL), lambda i: (0, i, 0))),
        scratch_shapes=[pltpu.VMEM((n, D_MODEL), F32),
                        pltpu.VMEM((n, D_MODEL), F32)],
        compiler_params=_params("arbitrary"),
        name="mix_conformer_sample",
    )(x2d, st, g, w1, b1, dw, dwb, lng, lnb, w2, b2)


def _sample_selector():
    nb = SAMPLE_BLOCK_C
    sel = np.zeros((4 * nb, nb * C_DV), np.float32)
    for p in range(3):
        for r in range(nb):
            sel[p * nb + r, r * C_DV:(r + 1) * C_DV] = 1.0
    return jnp.asarray(sel, BF16)


def _split3_rows(a):
    hi = a.astype(BF16).astype(F32)
    r1 = a - hi
    mid = r1.astype(BF16).astype(F32)
    lo = r1 - mid
    return jnp.concatenate([hi, mid, lo, jnp.zeros_like(a)], axis=0).astype(BF16)


def _mixc_sample_kernel(x_ref, s_ref, g_ref, wq_ref, lb_ref, gn_ref, wout_ref, sel_ref,
                        o_ref, so_ref, p_ref, orow_ref, om_ref, *, nb):
    i = pl.program_id(0)

    @pl.when(i == 0)
    def _():
        h = _rms(x_ref[...], g_ref[...]).astype(BF16)
        p_ref[...] = _dot(h, wq_ref[...])

    rows = pl.ds(pl.multiple_of(i * nb, nb), nb)
    d = D_MODEL
    f, kk = _gates(p_ref[rows, d:2 * d], lb_ref[...])
    q = _silu(p_ref[rows, 0:d]) * (C_DK ** -0.5)
    v = p_ref[rows, 2 * d:3 * d]

    for hd in range(C_HEADS):
        lanes = slice(hd * C_DK, (hd + 1) * C_DK)
        parts = jnp.concatenate([_split3_rows(a[:, lanes]) for a in (f, kk, q)], axis=1)
        cols = _dot_tn(parts, sel_ref[...])
        for r in range(nb):
            blk = slice(r * C_DV, (r + 1) * C_DV)
            s_new = cols[0:C_DK, blk] * s_ref[r, hd] + cols[C_DK:2 * C_DK, blk] * v[r:r + 1, lanes]
            so_ref[r, hd] = s_new
            orow_ref[r:r + 1, lanes] = jnp.sum(cols[2 * C_DK:3 * C_DK, blk] * s_new, axis=0,
                                               keepdims=True)

    gx = p_ref[rows, 3 * d:4 * d]
    for hd in range(C_HEADS):
        lanes = slice(hd * C_DK, (hd + 1) * C_DK)
        om_ref[rows, lanes] = _head_out(orow_ref[:, lanes], gx[:, lanes], gn_ref[...])

    @pl.when(i == pl.num_programs(0) - 1)
    def _():
        o_ref[...] = x_ref[...] + _dot(om_ref[...].astype(BF16), wout_ref[...])


def _mixc_sample(x2d, s, g, wq, lb_row, gn, wout, sel):
    n = x2d.shape[0]
    nb = SAMPLE_BLOCK_C
    return pl.pallas_call(
        functools.partial(_mixc_sample_kernel, nb=nb),
        out_shape=(jax.ShapeDtypeStruct((n, D_MODEL), F32),
                   jax.ShapeDtypeStruct((n, C_HEADS, C_DK, C_DV), F32)),
        grid=(n // nb,),
        in_specs=[_const_spec((n, D_MODEL)),
                  pl.BlockSpec((nb, C_HEADS, C_DK, C_DV), lambda i: (i, 0, 0, 0)),
                  _const_spec((1, D_MODEL)),
                  _const_spec((D_MODEL, 4 * D_MODEL)),
                  _const_spec((1, D_MODEL)),
                  _const_spec((1, C_DV)),
                  _const_spec((D_MODEL, D_MODEL)),
                  _const_spec(sel.shape)],
        out_specs=(pl.BlockSpec((n, D_MODEL), lambda i: (0, 0)),
                   pl.BlockSpec((nb, C_HEADS, C_DK, C_DV), lambda i: (i, 0, 0, 0))),
        scratch_shapes=[pltpu.VMEM((n, 4 * D_MODEL), F32),
                        pltpu.VMEM((nb, D_MODEL), F32),
                        pltpu.VMEM((n, D_MODEL), F32)],
        compiler_params=_params("arbitrary"),
        name="mix_hgrn2_sample",
    )(x2d, s, g, wq, lb_row, gn, wout, sel)


def _row(a):
    return a.reshape(1, -1)


def kernel(x_prompt, x_sample, state_conva, state_convb, state_hgrn, norm_mix, a_w_in, a_conv_w, a_w_out, b_w_pw1, b_b_pw1, b_dw_w, b_dw_b, b_ln_g, b_ln_b, b_w_pw2, b_b_pw2, c_lower_bounds, c_w_qfig, c_gnorm, c_w_out, norm_ffn, ffn_w_gate_up, ffn_w_down, norm_final):
    nb, t, d = x_prompt.shape
    ns = x_sample.shape[0]
    bf = lambda w: w.astype(BF16)
    a_w_in, a_w_out, b_w_pw1, b_w_pw2 = bf(a_w_in), bf(a_w_out), bf(b_w_pw1), bf(b_w_pw2)
    c_w_qfig, c_w_out = bf(c_w_qfig), bf(c_w_out)

    sm = jax.nn.softmax(c_lower_bounds.astype(F32), axis=0)
    lower = jnp.cumsum(sm, axis=0) - sm[0]
    a_stack, m_stack = _hgrn_constants()
    gfin = _row(norm_final)

    xp = x_prompt
    xs = x_sample.reshape(ns, d)
    g_ffn = norm_ffn.reshape(DEPTH, 1, d)
    conva_p, conva_s, convb_p, convb_s, hgrn_p, hgrn_s = [], [], [], [], [], []
    for i in range(DEPTH):
        kind, j = i % 3, i // 3
        g = _row(norm_mix[i])
        if kind == 0:
            w = (g, a_w_in[j], a_conv_w[j], a_w_out[j])
            xp, st = _mixa_prompt(xp, *w)
            conva_p.append(st)
            xs, st = _mixa_sample(xs, jnp.swapaxes(state_conva[j], 0, 1), *w)
            conva_s.append(jnp.swapaxes(st, 0, 1))
        elif kind == 1:
            w = (b_w_pw1[j], _row(b_b_pw1[j]))
            w2 = (_row(b_dw_b[j]), _row(b_ln_g[j]), _row(b_ln_b[j]), b_w_pw2[j], _row(b_b_pw2[j]))
            taps = jnp.broadcast_to(b_dw_w[j][::-1][:, None, :], (B_WIDTH, V7X_SUBLANES, d))
            xs, st = _mixb_sample(xs, jnp.swapaxes(state_convb, 1, 2), j, g, *w, b_dw_w[j], *w2)
            convb_s.append(jnp.swapaxes(st, 0, 1))
            xp, xs, st = _mixb_ffn(xp, xs, g, *w, taps, *w2, g_ffn, ffn_w_gate_up, ffn_w_down, gfin,
                                   layer=i, final=i == DEPTH - 1)
            convb_p.append(st)
            continue
        else:
            gn = _row(c_gnorm[j])
            xp, st = _mixc_prompt(xp, g, c_w_qfig[j], _row(lower[i]), gn, c_w_out[j], a_stack, m_stack)
            hgrn_p.append(st)
            xs, st = _mixc_sample(xs, state_hgrn[j], g, c_w_qfig[j], _row(lower[i]), gn, c_w_out[j],
                                  _sample_selector())
            hgrn_s.append(st)
        xp, xs = _ffn(xp.reshape(nb * t, d), xs, g_ffn, ffn_w_gate_up, ffn_w_down, gfin, layer=i,
                      final=i == DEPTH - 1)
        xp = xp.reshape(nb, t, d)
    stack = lambda xs_: xs_[0][None] if len(xs_) == 1 else jnp.stack(xs_)
    return (xp, xs.reshape(ns, 1, d), stack(conva_p), stack(conva_s), stack(convb_p),
            stack(convb_s), stack(hgrn_p), stack(hgrn_s))
```

```python
import functools

import numpy as np
import jax
import jax.numpy as jnp
from jax import lax
from jax.experimental import pallas as pl
from jax.experimental.pallas import tpu as pltpu

F32 = jnp.float32
BF16 = jnp.bfloat16

D_MODEL = 1024
DEPTH = 4
D_FF = 2816
A_WIDTH = 3
B_WIDTH = 31
C_HEADS = 8
C_DK = 128
C_DV = 128
NORM_EPS = 1e-6
LN_EPS = 1e-5

V7X_SUBLANES = 8
V7X_LANES = 128
V7X_VMEM_LIMIT_BYTES = 56 * 1024 * 1024

TOK_TILE = 512
FF_CHUNK = 256
HG_CHUNK = 128
HG_LEVELS = 7
SAMPLE_BLOCK_B = 16
SAMPLE_BLOCK_C = 8
A_HALO = V7X_SUBLANES
B_HALO = 32
CONV_ROWS = 64


def _params(*sem):
    return pltpu.CompilerParams(dimension_semantics=sem,
                                vmem_limit_bytes=V7X_VMEM_LIMIT_BYTES)


def _const_spec(shape):
    nd = len(shape)
    return pl.BlockSpec(shape, lambda *_: (0,) * nd, pipeline_mode=pl.Buffered(1))


def _layer_spec(shape, layer):
    nd = len(shape)
    return pl.BlockSpec((None,) + tuple(shape), lambda *_: (layer,) + (0,) * nd,
                        pipeline_mode=pl.Buffered(1))


def _rms(x, g):
    ms = jnp.mean(x * x, axis=-1, keepdims=True)
    return x * lax.rsqrt(ms + NORM_EPS) * g


def _silu(x):
    return x * jax.nn.sigmoid(x)


def _dot(a, b):
    return jnp.dot(a, b, preferred_element_type=F32)


def _dot_tn(a, b):
    return lax.dot_general(a, b, (((0,), (0,)), ((), ())), preferred_element_type=F32)


def _ffn_rows(x, g_ref, wgu_ref, wd_ref, gf_ref, h_ref, a_ref, final):
    h_ref[...] = _rms(x, g_ref[...]).astype(BF16)
    for j in range(D_FF // FF_CHUNK):
        lo = j * FF_CHUNK
        gate = _dot(h_ref[...], wgu_ref[:, lo:lo + FF_CHUNK])
        up = _dot(h_ref[...], wgu_ref[:, D_FF + lo:D_FF + lo + FF_CHUNK])
        a_ref[:, lo:lo + FF_CHUNK] = (_silu(gate) * up).astype(BF16)
    y = x + _dot(a_ref[...], wd_ref[...])
    return _rms(y, gf_ref[...]) if final else y


def _ffn_kernel(x_ref, xs_ref, g_ref, wgu_ref, wd_ref, gf_ref, o_ref, os_ref, h_ref, a_ref, *, final):
    args = (g_ref, wgu_ref, wd_ref, gf_ref)
    o_ref[...] = _ffn_rows(x_ref[...], *args, h_ref, a_ref, final)

    @pl.when(pl.program_id(0) == pl.num_programs(0) - 1)
    def _():
        ns = xs_ref.shape[0]
        os_ref[...] = _ffn_rows(xs_ref[...], *args, h_ref.at[0:ns], a_ref.at[0:ns], final)


def _ffn(x2d, xs, g3, wgu, wd, gf, *, layer, final):
    m = x2d.shape[0]
    ns = xs.shape[0]
    tm = TOK_TILE
    return pl.pallas_call(
        functools.partial(_ffn_kernel, final=final),
        out_shape=(jax.ShapeDtypeStruct((m, D_MODEL), F32),
                   jax.ShapeDtypeStruct((ns, D_MODEL), F32)),
        grid=(m // tm,),
        in_specs=[pl.BlockSpec((tm, D_MODEL), lambda i: (i, 0)),
                  _const_spec((ns, D_MODEL)),
                  _layer_spec((1, D_MODEL), layer),
                  _const_spec((D_MODEL, 2 * D_FF)),
                  _const_spec((D_FF, D_MODEL)),
                  _const_spec((1, D_MODEL))],
        out_specs=(pl.BlockSpec((tm, D_MODEL), lambda i: (i, 0)),
                   pl.BlockSpec((ns, D_MODEL), lambda i: (0, 0))),
        scratch_shapes=[pltpu.VMEM((tm, D_MODEL), BF16),
                        pltpu.VMEM((tm, D_FF), BF16)],
        compiler_params=_params("arbitrary"),
        name="ffn",
    )(x2d, xs, g3, wgu, wd, gf)


def _stage_plan(nsteps):
    nd = nsteps // 2
    assert D_MODEL % nsteps == 0 and D_FF % nd == 0
    return D_MODEL // nsteps, D_FF // nd


def _stage_specs(layer, n, tps):
    ru, rd = _stage_plan(n * tps)
    in_specs = [pl.BlockSpec((None, ru, 2 * D_FF), lambda i, j: (layer, i * tps + j, 0)),
                pl.BlockSpec((None, rd, D_MODEL), lambda i, j: (layer, (i * tps + j) // 2, 0))]
    out_specs = (pl.BlockSpec((ru, 2 * D_FF), lambda i, j: (i * tps + j, 0)),
                 pl.BlockSpec((rd, D_MODEL), lambda i, j: ((i * tps + j) // 2, 0)))
    out_shape = (jax.ShapeDtypeStruct((D_MODEL, 2 * D_FF), BF16),
                 jax.ShapeDtypeStruct((D_FF, D_MODEL), BF16))
    return in_specs, out_specs, out_shape


def _stage(wgu_in_ref, wd_in_ref, wgu_out_ref, wd_out_ref):
    wgu_out_ref[...] = wgu_in_ref[...].astype(BF16)
    wd_out_ref[...] = wd_in_ref[...].astype(BF16)


def _carry_halo(ubuf_ref, halo, tt):
    j = pl.program_id(1)

    @pl.when(j == 0)
    def _():
        ubuf_ref[0:halo, :] = jnp.zeros((halo, D_MODEL), F32)

    @pl.when(j > 0)
    def _():
        ubuf_ref[0:halo, :] = ubuf_ref[tt:tt + halo, :]


def _mixa_kernel(x_ref, g_ref, win_ref, cw_ref, wout_ref, wgu_in_ref, wd_in_ref,
                 o_ref, st_ref, wgu_out_ref, wd_out_ref, h_ref, ubuf_ref, *, tt):
    x = x_ref[...]
    h_ref[...] = _rms(x, g_ref[...]).astype(BF16)
    _carry_halo(ubuf_ref, A_HALO, tt)
    cg = _dot(h_ref[...], win_ref[:, D_MODEL:2 * D_MODEL])
    hv = _dot(h_ref[...], win_ref[:, 2 * D_MODEL:3 * D_MODEL])
    ubuf_ref[A_HALO:A_HALO + tt, :] = cg * hv
    y = cw_ref[A_WIDTH - 1:A_WIDTH, :] * ubuf_ref[A_HALO:A_HALO + tt, :]
    for k in range(A_WIDTH - 1):
        off = A_HALO - (A_WIDTH - 1) + k
        y = y + cw_ref[k:k + 1, :] * ubuf_ref[off:off + tt, :]
    bg = _dot(h_ref[...], win_ref[:, 0:D_MODEL])
    o_ref[...] = x + _dot((bg * y).astype(BF16), wout_ref[...])
    _stage(wgu_in_ref, wd_in_ref, wgu_out_ref, wd_out_ref)

    @pl.when(pl.program_id(1) == pl.num_programs(1) - 1)
    def _():
        st_ref[...] = ubuf_ref[A_HALO + tt - (A_WIDTH - 1):A_HALO + tt, :]


def _mixa_prompt(x, g, win, cw, wout, wgu, wd, layer):
    n, t, _ = x.shape
    tt = TOK_TILE
    st_in, st_out, st_shape = _stage_specs(layer, n, t // tt)
    return pl.pallas_call(
        functools.partial(_mixa_kernel, tt=tt),
        out_shape=(jax.ShapeDtypeStruct((n, t, D_MODEL), F32),
                   jax.ShapeDtypeStruct((n, A_WIDTH - 1, D_MODEL), F32)) + st_shape,
        grid=(n, t // tt),
        in_specs=[pl.BlockSpec((None, tt, D_MODEL), lambda i, j: (i, j, 0)),
                  _const_spec((1, D_MODEL)),
                  _const_spec((D_MODEL, 3 * D_MODEL)),
                  _const_spec((A_WIDTH, D_MODEL)),
                  _const_spec((D_MODEL, D_MODEL))] + st_in,
        out_specs=(pl.BlockSpec((None, tt, D_MODEL), lambda i, j: (i, j, 0)),
                   pl.BlockSpec((None, A_WIDTH - 1, D_MODEL), lambda i, j: (i, 0, 0))) + st_out,
        scratch_shapes=[pltpu.VMEM((tt, D_MODEL), BF16),
                        pltpu.VMEM((A_HALO + tt, D_MODEL), F32)],
        compiler_params=_params("arbitrary", "arbitrary"),
        name="mix_shortconv",
    )(x, g, win, cw, wout, wgu, wd)


def _layer_norm(y, g, b):
    mu = jnp.mean(y, axis=-1, keepdims=True)
    yc = y - mu
    var = jnp.mean(yc * yc, axis=-1, keepdims=True)
    return yc * lax.rsqrt(var + LN_EPS) * g + b


def _dwconv_tile(ubuf_ref, taps_ref, y_ref, r0):
    sub = V7X_SUBLANES
    nblk = CONV_ROWS // sub
    na = -(-B_WIDTH // sub)
    assert B_HALO == na * sub
    rowid = lax.broadcasted_iota(jnp.int32, (sub, V7X_LANES), 0)
    for lt in range(D_MODEL // V7X_LANES):
        lanes = slice(lt * V7X_LANES, (lt + 1) * V7X_LANES)
        ub = [ubuf_ref[pl.ds(r0 + sub * i, sub), lanes] for i in range(nblk + na)]
        acc = None
        for r in range(sub):
            z = []
            for m in range(nblk + 1):
                zm = None
                for a, s in enumerate(range(r, B_WIDTH, sub)):
                    term = taps_ref[s, :, lanes] * ub[na - 1 + m - a]
                    zm = term if zm is None else zm + term
                z.append(zm)
            if r == 0:
                shifted = z[1:]
            else:
                rolled = [pltpu.roll(zm, r, 0) for zm in z]
                shifted = [jnp.where(rowid >= r, rolled[n + 1], rolled[n]) for n in range(nblk)]
            acc = shifted if acc is None else [p + c for p, c in zip(acc, shifted)]
        for n in range(nblk):
            y_ref[pl.ds(r0 + sub * n, sub), lanes] = acc[n]


def _mixb_kernel(x_ref, g_ref, w1_ref, b1_ref, taps_ref, dwb_ref, lng_ref, lnb_ref, w2_ref, b2_ref,
                 wgu_in_ref, wd_in_ref, o_ref, st_ref, wgu_out_ref, wd_out_ref,
                 h_ref, ubuf_ref, y_ref, *, tt):
    x = x_ref[...]
    h_ref[...] = _rms(x, g_ref[...]).astype(BF16)
    _carry_halo(ubuf_ref, B_HALO, tt)
    a = _dot(h_ref[...], w1_ref[:, 0:D_MODEL]) + b1_ref[:, 0:D_MODEL]
    gate = _dot(h_ref[...], w1_ref[:, D_MODEL:2 * D_MODEL]) + b1_ref[:, D_MODEL:2 * D_MODEL]
    ubuf_ref[B_HALO:B_HALO + tt, :] = a * jax.nn.sigmoid(gate)

    def conv_body(i, carry):
        _dwconv_tile(ubuf_ref, taps_ref, y_ref, pl.multiple_of(i * CONV_ROWS, CONV_ROWS))
        return carry

    lax.fori_loop(0, tt // CONV_ROWS, conv_body, 0)
    z = _silu(_layer_norm(y_ref[...] + dwb_ref[...], lng_ref[...], lnb_ref[...]))
    o_ref[...] = x + _dot(z.astype(BF16), w2_ref[...]) + b2_ref[...]
    _stage(wgu_in_ref, wd_in_ref, wgu_out_ref, wd_out_ref)

    @pl.when(pl.program_id(1) == pl.num_programs(1) - 1)
    def _():
        st_ref[...] = ubuf_ref[B_HALO + tt - (B_WIDTH - 1):B_HALO + tt, :]


def _mixb_prompt(x, g, w1, b1, taps, dwb, lng, lnb, w2, b2, wgu, wd, layer):
    n, t, _ = x.shape
    tt = TOK_TILE
    st_in, st_out, st_shape = _stage_specs(layer, n, t // tt)
    return pl.pallas_call(
        functools.partial(_mixb_kernel, tt=tt),
        out_shape=(jax.ShapeDtypeStruct((n, t, D_MODEL), F32),
                   jax.ShapeDtypeStruct((n, B_WIDTH - 1, D_MODEL), F32)) + st_shape,
        grid=(n, t // tt),
        in_specs=[pl.BlockSpec((None, tt, D_MODEL), lambda i, j: (i, j, 0)),
                  _const_spec((1, D_MODEL)),
                  _const_spec((D_MODEL, 2 * D_MODEL)),
                  _const_spec((1, 2 * D_MODEL)),
                  _const_spec((B_WIDTH, V7X_SUBLANES, D_MODEL)),
                  _const_spec((1, D_MODEL)),
                  _const_spec((1, D_MODEL)),
                  _const_spec((1, D_MODEL)),
                  _const_spec((D_MODEL, D_MODEL)),
                  _const_spec((1, D_MODEL))] + st_in,
        out_specs=(pl.BlockSpec((None, tt, D_MODEL), lambda i, j: (i, j, 0)),
                   pl.BlockSpec((None, B_WIDTH - 1, D_MODEL), lambda i, j: (i, 0, 0))) + st_out,
        scratch_shapes=[pltpu.VMEM((tt, D_MODEL), BF16),
                        pltpu.VMEM((B_HALO + tt, D_MODEL), F32),
                        pltpu.VMEM((tt, D_MODEL), F32)],
        compiler_params=_params("arbitrary", "arbitrary"),
        name="mix_conformer",
    )(x, g, w1, b1, taps, dwb, lng, lnb, w2, b2, wgu, wd)


def _hgrn_constants():
    c = HG_CHUNK
    t = np.arange(c)[:, None]
    r = np.arange(c)[None, :]
    blocks = [(r <= t), (r > t)]
    masks = [(t == r)]
    for lvl in range(HG_LEVELS):
        h = 1 << lvl
        m = (t // (2 * h)) * (2 * h) + h - 1
        upper = (t & h) != 0
        blocks.append(np.where(upper, (r > m) & (r <= t), (r > t) & (r <= m)))
        masks.append((((t ^ r) >> lvl) == 1) & upper)
    a_stack = np.concatenate(blocks, axis=0).astype(np.float32)
    a_stack = np.concatenate([a_stack, a_stack], axis=1)
    m_stack = np.stack(masks, axis=0).astype(np.float32)
    return jnp.asarray(a_stack, BF16), jnp.asarray(m_stack, BF16)


def _gates(fx, lb):
    sig = jax.nn.sigmoid(fx)
    return lb + (1.0 - lb) * sig, (1.0 - lb) * (1.0 - sig)


def _head_out(o, gx, gn):
    o = o * lax.rsqrt(jnp.mean(o * o, axis=-1, keepdims=True) + NORM_EPS) * gn
    return o * _silu(gx)


def _hgrn_head_chunk(q, kk, v, s, e, m_ref):
    c = HG_CHUNK
    e_b = e(0)
    vb = v.astype(BF16)
    o = _dot((q * e_b).astype(BF16), s.astype(BF16))
    scores = _dot(q.astype(BF16), kk.T.astype(BF16)).astype(BF16) * m_ref[0]
    for lvl in range(HG_LEVELS):
        e_l = e(2 + lvl)
        p_l = _dot((q * e_l).astype(BF16), (kk * e_l).T.astype(BF16))
        scores = scores + p_l.astype(BF16) * m_ref[1 + lvl]
    o = o + _dot(scores, vb)
    decay = jnp.broadcast_to(e_b[c - 1:c, :], (C_DK, C_DK)).T
    s_new = s * decay + _dot_tn((kk * e(1)).astype(BF16), vb)
    return o, s_new


def _mixc_kernel(x_ref, g_ref, wq_ref, lb_ref, gn_ref, wout_ref, a_ref, m_ref, wgu_in_ref, wd_in_ref,
                 o_ref, sout_ref, wgu_out_ref, wd_out_ref,
                 h_ref, p_ref, qk_ref, e_ref, om_ref, s_ref, *, tt):
    j = pl.program_id(1)
    d = D_MODEL
    c = HG_CHUNK

    @pl.when(j == 0)
    def _():
        s_ref[...] = jnp.zeros_like(s_ref)

    x = x_ref[...]
    h_ref[...] = _rms(x, g_ref[...]).astype(BF16)
    nb = 4 * V7X_LANES
    for b in range(4 * d // nb):
        p_ref[:, b * nb:(b + 1) * nb] = _dot(h_ref[...], wq_ref[:, b * nb:(b + 1) * nb])

    def prep(ci):
        rows = slice(ci * c, (ci + 1) * c)
        f, kk = _gates(p_ref[rows, d:2 * d], lb_ref[...])
        log2f = jnp.log2(f)
        hi = log2f.astype(BF16)
        lo = (log2f - hi.astype(F32)).astype(BF16)
        e_ref[ci % 2] = jnp.exp2(_dot(a_ref[...], jnp.concatenate([hi, lo], axis=0)))
        qk_ref[ci % 2, :, 0:d] = _silu(p_ref[rows, 0:d]) * (C_DK ** -0.5)
        qk_ref[ci % 2, :, d:2 * d] = kk

    def heads(ci):
        rows = slice(ci * c, (ci + 1) * c)
        slot = ci % 2
        for hd in range(C_HEADS):
            lanes = slice(hd * C_DK, (hd + 1) * C_DK)
            o, s_ref[hd] = _hgrn_head_chunk(
                qk_ref[slot, :, lanes], qk_ref[slot, :, d + hd * C_DK:d + (hd + 1) * C_DK],
                p_ref[rows, 2 * d + hd * C_DV:2 * d + (hd + 1) * C_DV], s_ref[hd],
                lambda r: e_ref[slot, r * c:(r + 1) * c, lanes], m_ref)
            gx = p_ref[rows, 3 * d + hd * C_DV:3 * d + (hd + 1) * C_DV]
            om_ref[rows, lanes] = _head_out(o, gx, gn_ref[...]).astype(BF16)

    nchunk = tt // c
    prep(0)
    for ci in range(nchunk):
        if ci + 1 < nchunk:
            prep(ci + 1)
        heads(ci)
    o_ref[...] = x + _dot(om_ref[...], wout_ref[...])
    _stage(wgu_in_ref, wd_in_ref, wgu_out_ref, wd_out_ref)

    @pl.when(j == pl.num_programs(1) - 1)
    def _():
        sout_ref[...] = s_ref[...]


def _mixc_prompt(x, g, wq, lb_row, gn, wout, a_stack, m_stack, wgu, wd, layer):
    n, t, _ = x.shape
    tt = TOK_TILE
    st_in, st_out, st_shape = _stage_specs(layer, n, t // tt)
    return pl.pallas_call(
        functools.partial(_mixc_kernel, tt=tt),
        out_shape=(jax.ShapeDtypeStruct((n, t, D_MODEL), F32),
                   jax.ShapeDtypeStruct((n, C_HEADS, C_DK, C_DV), F32)) + st_shape,
        grid=(n, t // tt),
        in_specs=[pl.BlockSpec((None, tt, D_MODEL), lambda i, j: (i, j, 0)),
                  _const_spec((1, D_MODEL)),
                  _const_spec((D_MODEL, 4 * D_MODEL)),
                  _const_spec((1, D_MODEL)),
                  _const_spec((1, C_DV)),
                  _const_spec((D_MODEL, D_MODEL)),
                  _const_spec(a_stack.shape),
                  _const_spec(m_stack.shape)] + st_in,
        out_specs=(pl.BlockSpec((None, tt, D_MODEL), lambda i, j: (i, j, 0)),
                   pl.BlockSpec((None, C_HEADS, C_DK, C_DV), lambda i, j: (i, 0, 0, 0))) + st_out,
        scratch_shapes=[pltpu.VMEM((tt, D_MODEL), BF16),
                        pltpu.VMEM((tt, 4 * D_MODEL), F32),
                        pltpu.VMEM((2, HG_CHUNK, 2 * D_MODEL), F32),
                        pltpu.VMEM((2, (2 + HG_LEVELS) * HG_CHUNK, D_MODEL), F32),
                        pltpu.VMEM((tt, D_MODEL), BF16),
                        pltpu.VMEM((C_HEADS, C_DK, C_DV), F32)],
        compiler_params=_params("arbitrary", "arbitrary"),
        name="mix_hgrn2",
    )(x, g, wq, lb_row, gn, wout, a_stack, m_stack, wgu, wd)


def _mixa_sample_kernel(x_ref, st_ref, g_ref, win_ref, cw_ref, wout_ref, o_ref, sto_ref):
    x = x_ref[...]
    h = _rms(x, g_ref[...]).astype(BF16)
    p = _dot(h, win_ref[...])
    bg = p[:, 0:D_MODEL]
    u = p[:, D_MODEL:2 * D_MODEL] * p[:, 2 * D_MODEL:3 * D_MODEL]
    y = cw_ref[0:1, :] * st_ref[0] + cw_ref[1:2, :] * st_ref[1] + cw_ref[2:3, :] * u
    o_ref[...] = x + _dot((bg * y).astype(BF16), wout_ref[...])
    sto_ref[0] = st_ref[1]
    sto_ref[1] = u


def _mixa_sample(x2d, st, g, win, cw, wout):
    n = x2d.shape[0]
    return pl.pallas_call(
        _mixa_sample_kernel,
        out_shape=(jax.ShapeDtypeStruct((n, D_MODEL), F32),
                   jax.ShapeDtypeStruct((A_WIDTH - 1, n, D_MODEL), F32)),
        grid=(1,),
        in_specs=[_const_spec((n, D_MODEL)),
                  _const_spec((A_WIDTH - 1, n, D_MODEL)),
                  _const_spec((1, D_MODEL)),
                  _const_spec((D_MODEL, 3 * D_MODEL)),
                  _const_spec((A_WIDTH, D_MODEL)),
                  _const_spec((D_MODEL, D_MODEL))],
        out_specs=(pl.BlockSpec((n, D_MODEL), lambda i: (0, 0)),
                   pl.BlockSpec((A_WIDTH - 1, n, D_MODEL), lambda i: (0, 0, 0))),
        compiler_params=_params("arbitrary"),
        name="mix_shortconv_sample",
    )(x2d, st, g, win, cw, wout)


def _mixb_sample_kernel(x_ref, st_ref, g_ref, w1_ref, b1_ref, dw_ref, dwb_ref, lng_ref, lnb_ref,
                        w2_ref, b2_ref, o_ref, sto_ref, u_ref, y_ref, *, nb):
    i = pl.program_id(0)
    hist = B_WIDTH - 1

    @pl.when(i == 0)
    def _():
        h = _rms(x_ref[...], g_ref[...]).astype(BF16)
        p = _dot(h, w1_ref[...]) + b1_ref[...]
        u_ref[...] = p[:, 0:D_MODEL] * jax.nn.sigmoid(p[:, D_MODEL:2 * D_MODEL])

    rows = pl.ds(pl.multiple_of(i * nb, nb), nb)
    u = u_ref[rows, :]
    y = dw_ref[hist:hist + 1, :] * u
    for k in range(hist):
        y = y + dw_ref[k:k + 1, :] * st_ref[k]
        sto_ref[k] = st_ref[k + 1] if k + 1 < hist else u
    y_ref[rows, :] = y

    @pl.when(i == pl.num_programs(0) - 1)
    def _():
        z = _silu(_layer_norm(y_ref[...] + dwb_ref[...], lng_ref[...], lnb_ref[...]))
        o_ref[...] = x_ref[...] + _dot(z.astype(BF16), w2_ref[...]) + b2_ref[...]


def _mixb_sample(x2d, st, layer, g, w1, b1, dw, dwb, lng, lnb, w2, b2):
    n = x2d.shape[0]
    nb = SAMPLE_BLOCK_B
    hist = B_WIDTH - 1
    return pl.pallas_call(
        functools.partial(_mixb_sample_kernel, nb=nb),
        out_shape=(jax.ShapeDtypeStruct((n, D_MODEL), F32),
                   jax.ShapeDtypeStruct((hist, n, D_MODEL), F32)),
        grid=(n // nb,),
        in_specs=[_const_spec((n, D_MODEL)),
                  pl.BlockSpec((None, hist, nb, D_MODEL), lambda i: (layer, 0, i, 0)),
                  _const_spec((1, D_MODEL)),
                  _const_spec((D_MODEL, 2 * D_MODEL)),
                  _const_spec((1, 2 * D_MODEL)),
                  _const_spec((B_WIDTH, D_MODEL)),
                  _const_spec((1, D_MODEL)),
                  _const_spec((1, D_MODEL)),
                  _const_spec((1, D_MODEL)),
                  _const_spec((D_MODEL, D_MODEL)),
                  _const_spec((1, D_MODEL))],
        out_specs=(pl.BlockSpec((n, D_MODEL), lambda i: (0, 0)),
                   pl.BlockSpec((hist, nb, D_MODEL), lambda i: (0, i, 0))),
        scratch_shapes=[pltpu.VMEM((n, D_MODEL), F32),
                        pltpu.VMEM((n, D_MODEL), F32)],
        compiler_params=_params("arbitrary"),
        name="mix_conformer_sample",
    )(x2d, st, g, w1, b1, dw, dwb, lng, lnb, w2, b2)


def _sample_selector():
    nb = SAMPLE_BLOCK_C
    sel = np.zeros((4 * nb, nb * C_DV), np.float32)
    for p in range(3):
        for r in range(nb):
            sel[p * nb + r, r * C_DV:(r + 1) * C_DV] = 1.0
    return jnp.asarray(sel, BF16)


def _split3_rows(a):
    hi = a.astype(BF16).astype(F32)
    r1 = a - hi
    mid = r1.astype(BF16).astype(F32)
    lo = r1 - mid
    return jnp.concatenate([hi, mid, lo, jnp.zeros_like(a)], axis=0).astype(BF16)


def _mixc_sample_kernel(x_ref, s_ref, g_ref, wq_ref, lb_ref, gn_ref, wout_ref, sel_ref,
                        o_ref, so_ref, p_ref, orow_ref, om_ref, *, nb):
    i = pl.program_id(0)

    @pl.when(i == 0)
    def _():
        h = _rms(x_ref[...], g_ref[...]).astype(BF16)
        p_ref[...] = _dot(h, wq_ref[...])

    rows = pl.ds(pl.multiple_of(i * nb, nb), nb)
    d = D_MODEL
    f, kk = _gates(p_ref[rows, d:2 * d], lb_ref[...])
    q = _silu(p_ref[rows, 0:d]) * (C_DK ** -0.5)
    v = p_ref[rows, 2 * d:3 * d]

    for hd in range(C_HEADS):
        lanes = slice(hd * C_DK, (hd + 1) * C_DK)
        parts = jnp.concatenate([_split3_rows(a[:, lanes]) for a in (f, kk, q)], axis=1)
        cols = _dot_tn(parts, sel_ref[...])
        for r in range(nb):
            blk = slice(r * C_DV, (r + 1) * C_DV)
            s_new = cols[0:C_DK, blk] * s_ref[r, hd] + cols[C_DK:2 * C_DK, blk] * v[r:r + 1, lanes]
            so_ref[r, hd] = s_new
            orow_ref[r:r + 1, lanes] = jnp.sum(cols[2 * C_DK:3 * C_DK, blk] * s_new, axis=0,
                                               keepdims=True)

    gx = p_ref[rows, 3 * d:4 * d]
    for hd in range(C_HEADS):
        lanes = slice(hd * C_DK, (hd + 1) * C_DK)
        om_ref[rows, lanes] = _head_out(orow_ref[:, lanes], gx[:, lanes], gn_ref[...])

    @pl.when(i == pl.num_programs(0) - 1)
    def _():
        o_ref[...] = x_ref[...] + _dot(om_ref[...].astype(BF16), wout_ref[...])


def _mixc_sample(x2d, s, g, wq, lb_row, gn, wout, sel):
    n = x2d.shape[0]
    nb = SAMPLE_BLOCK_C
    return pl.pallas_call(
        functools.partial(_mixc_sample_kernel, nb=nb),
        out_shape=(jax.ShapeDtypeStruct((n, D_MODEL), F32),
                   jax.ShapeDtypeStruct((n, C_HEADS, C_DK, C_DV), F32)),
        grid=(n // nb,),
        in_specs=[_const_spec((n, D_MODEL)),
                  pl.BlockSpec((nb, C_HEADS, C_DK, C_DV), lambda i: (i, 0, 0, 0)),
                  _const_spec((1, D_MODEL)),
                  _const_spec((D_MODEL, 4 * D_MODEL)),
                  _const_spec((1, D_MODEL)),
                  _const_spec((1, C_DV)),
                  _const_spec((D_MODEL, D_MODEL)),
                  _const_spec(sel.shape)],
        out_specs=(pl.BlockSpec((n, D_MODEL), lambda i: (0, 0)),
                   pl.BlockSpec((nb, C_HEADS, C_DK, C_DV), lambda i: (i, 0, 0, 0))),
        scratch_shapes=[pltpu.VMEM((n, 4 * D_MODEL), F32),
                        pltpu.VMEM((nb, D_MODEL), F32),
                        pltpu.VMEM((n, D_MODEL), F32)],
        compiler_params=_params("arbitrary"),
        name="mix_hgrn2_sample",
    )(x2d, s, g, wq, lb_row, gn, wout, sel)


def _row(a):
    return a.reshape(1, -1)


def kernel(x_prompt, x_sample, state_conva, state_convb, state_hgrn, norm_mix, a_w_in, a_conv_w, a_w_out, b_w_pw1, b_b_pw1, b_dw_w, b_dw_b, b_ln_g, b_ln_b, b_w_pw2, b_b_pw2, c_lower_bounds, c_w_qfig, c_gnorm, c_w_out, norm_ffn, ffn_w_gate_up, ffn_w_down, norm_final):
    nb, t, d = x_prompt.shape
    ns = x_sample.shape[0]
    bf = lambda w: w.astype(BF16)
    a_w_in, a_w_out, b_w_pw1, b_w_pw2 = bf(a_w_in), bf(a_w_out), bf(b_w_pw1), bf(b_w_pw2)
    c_w_qfig, c_w_out = bf(c_w_qfig), bf(c_w_out)

    sm = jax.nn.softmax(c_lower_bounds.astype(F32), axis=0)
    lower = jnp.cumsum(sm, axis=0) - sm[0]
    a_stack, m_stack = _hgrn_constants()
    gfin = _row(norm_final)

    xp = x_prompt
    xs = x_sample.reshape(ns, d)
    g_ffn = norm_ffn.reshape(DEPTH, 1, d)
    conva_p, conva_s, convb_p, convb_s, hgrn_p, hgrn_s = [], [], [], [], [], []
    for i in range(DEPTH):
        kind, j = i % 3, i // 3
        g = _row(norm_mix[i])
        if kind == 0:
            w = (g, a_w_in[j], a_conv_w[j], a_w_out[j])
            xp, st, wgu, wd = _mixa_prompt(xp, *w, ffn_w_gate_up, ffn_w_down, i)
            conva_p.append(st)
            xs, st = _mixa_sample(xs, jnp.swapaxes(state_conva[j], 0, 1), *w)
            conva_s.append(jnp.swapaxes(st, 0, 1))
        elif kind == 1:
            w = (b_w_pw1[j], _row(b_b_pw1[j]))
            w2 = (_row(b_dw_b[j]), _row(b_ln_g[j]), _row(b_ln_b[j]), b_w_pw2[j], _row(b_b_pw2[j]))
            taps = jnp.broadcast_to(b_dw_w[j][::-1][:, None, :], (B_WIDTH, V7X_SUBLANES, d))
            xp, st, wgu, wd = _mixb_prompt(xp, g, *w, taps, *w2, ffn_w_gate_up, ffn_w_down, i)
            convb_p.append(st)
            xs, st = _mixb_sample(xs, jnp.swapaxes(state_convb, 1, 2), j, g, *w, b_dw_w[j], *w2)
            convb_s.append(jnp.swapaxes(st, 0, 1))
        else:
            gn = _row(c_gnorm[j])
            xp, st, wgu, wd = _mixc_prompt(xp, g, c_w_qfig[j], _row(lower[i]), gn, c_w_out[j], a_stack,
                                           m_stack, ffn_w_gate_up, ffn_w_down, i)
            hgrn_p.append(st)
            xs, st = _mixc_sample(xs, state_hgrn[j], g, c_w_qfig[j], _row(lower[i]), gn, c_w_out[j],
                                  _sample_selector())
            hgrn_s.append(st)
        xp, xs = _ffn(xp.reshape(nb * t, d), xs, g_ffn, wgu, wd, gfin, layer=i, final=i == DEPTH - 1)
        xp = xp.reshape(nb, t, d)
    stack = lambda xs_: xs_[0][None] if len(xs_) == 1 else jnp.stack(xs_)
    return (xp, xs.reshape(ns, 1, d), stack(conva_p), stack(conva_s), stack(convb_p),
            stack(convb_s), stack(hgrn_p), stack(hgrn_s))
```

```python
import functools

import numpy as np
import jax
import jax.numpy as jnp
from jax import lax
from jax.experimental import pallas as pl
from jax.experimental.pallas import tpu as pltpu

F32 = jnp.float32
BF16 = jnp.bfloat16

D_MODEL = 1024
DEPTH = 4
D_FF = 2816
A_WIDTH = 3
B_WIDTH = 31
C_HEADS = 8
C_DK = 128
C_DV = 128
NORM_EPS = 1e-6
LN_EPS = 1e-5

V7X_SUBLANES = 8
V7X_LANES = 128
V7X_VMEM_LIMIT_BYTES = 56 * 1024 * 1024

TOK_TILE = 512
A_TOK_TILE = 1024
FF_CHUNK = 256
HG_CHUNK = 128
HG_LEVELS = 7
SAMPLE_BLOCK_B = 16
SAMPLE_BLOCK_C = 8
A_HALO = V7X_SUBLANES
B_HALO = 32
CONV_ROWS = 128


def _params(*sem):
    return pltpu.CompilerParams(dimension_semantics=sem,
                                vmem_limit_bytes=V7X_VMEM_LIMIT_BYTES)


def _const_spec(shape):
    nd = len(shape)
    return pl.BlockSpec(shape, lambda *_: (0,) * nd, pipeline_mode=pl.Buffered(1))


def _layer_spec(shape, layer):
    nd = len(shape)
    return pl.BlockSpec((None,) + tuple(shape), lambda *_: (layer,) + (0,) * nd,
                        pipeline_mode=pl.Buffered(1))


def _rms(x, g):
    ms = jnp.mean(x * x, axis=-1, keepdims=True)
    return x * lax.rsqrt(ms + NORM_EPS) * g


def _silu(x):
    return x * jax.nn.sigmoid(x)


def _dot(a, b):
    return jnp.dot(a, b, preferred_element_type=F32)


def _dot_tn(a, b):
    return lax.dot_general(a, b, (((0,), (0,)), ((), ())), preferred_element_type=F32)


def _ffn_rows(x, g_ref, wgu_ref, wd_ref, gf_ref, h_ref, a_ref, final):
    h_ref[...] = _rms(x, g_ref[...]).astype(BF16)
    for j in range(D_FF // FF_CHUNK):
        lo = j * FF_CHUNK
        gate = _dot(h_ref[...], wgu_ref[:, lo:lo + FF_CHUNK])
        up = _dot(h_ref[...], wgu_ref[:, D_FF + lo:D_FF + lo + FF_CHUNK])
        a_ref[:, lo:lo + FF_CHUNK] = (_silu(gate) * up).astype(BF16)
    y = x + _dot(a_ref[...], wd_ref[...])
    return _rms(y, gf_ref[...]) if final else y


def _ffn_kernel(x_ref, xs_ref, g_ref, wgu_ref, wd_ref, gf_ref, o_ref, os_ref, h_ref, a_ref, *, final):
    args = (g_ref, wgu_ref, wd_ref, gf_ref)
    o_ref[...] = _ffn_rows(x_ref[...], *args, h_ref, a_ref, final)

    @pl.when(pl.program_id(0) == pl.num_programs(0) - 1)
    def _():
        ns = xs_ref.shape[0]
        os_ref[...] = _ffn_rows(xs_ref[...], *args, h_ref.at[0:ns], a_ref.at[0:ns], final)


def _ffn(x2d, xs, g3, wgu, wd, gf, *, layer, final):
    m = x2d.shape[0]
    ns = xs.shape[0]
    tm = TOK_TILE
    return pl.pallas_call(
        functools.partial(_ffn_kernel, final=final),
        out_shape=(jax.ShapeDtypeStruct((m, D_MODEL), F32),
                   jax.ShapeDtypeStruct((ns, D_MODEL), F32)),
        grid=(m // tm,),
        in_specs=[pl.BlockSpec((tm, D_MODEL), lambda i: (i, 0)),
                  _const_spec((ns, D_MODEL)),
                  _layer_spec((1, D_MODEL), layer),
                  _const_spec((D_MODEL, 2 * D_FF)),
                  _const_spec((D_FF, D_MODEL)),
                  _const_spec((1, D_MODEL))],
        out_specs=(pl.BlockSpec((tm, D_MODEL), lambda i: (i, 0)),
                   pl.BlockSpec((ns, D_MODEL), lambda i: (0, 0))),
        scratch_shapes=[pltpu.VMEM((tm, D_MODEL), BF16),
                        pltpu.VMEM((tm, D_FF), BF16)],
        compiler_params=_params("arbitrary"),
        name="ffn",
    )(x2d, xs, g3, wgu, wd, gf)


def _stage_plan(nsteps):
    nd = nsteps // 2
    assert D_MODEL % nsteps == 0 and D_FF % nd == 0
    return D_MODEL // nsteps, D_FF // nd


def _stage_specs(layer, n, tps):
    ru, rd = _stage_plan(n * tps)
    in_specs = [pl.BlockSpec((None, ru, 2 * D_FF), lambda i, j: (layer, i * tps + j, 0)),
                pl.BlockSpec((None, rd, D_MODEL), lambda i, j: (layer, (i * tps + j) // 2, 0))]
    out_specs = (pl.BlockSpec((ru, 2 * D_FF), lambda i, j: (i * tps + j, 0)),
                 pl.BlockSpec((rd, D_MODEL), lambda i, j: ((i * tps + j) // 2, 0)))
    out_shape = (jax.ShapeDtypeStruct((D_MODEL, 2 * D_FF), BF16),
                 jax.ShapeDtypeStruct((D_FF, D_MODEL), BF16))
    return in_specs, out_specs, out_shape


def _stage(wgu_in_ref, wd_in_ref, wgu_out_ref, wd_out_ref):
    wgu_out_ref[...] = wgu_in_ref[...].astype(BF16)
    wd_out_ref[...] = wd_in_ref[...].astype(BF16)


def _carry_halo(ubuf_ref, halo, tt):
    j = pl.program_id(1)

    @pl.when(j == 0)
    def _():
        ubuf_ref[0:halo, :] = jnp.zeros((halo, D_MODEL), F32)

    @pl.when(j > 0)
    def _():
        ubuf_ref[0:halo, :] = ubuf_ref[tt:tt + halo, :]


def _mixa_kernel(x_ref, g_ref, win_ref, cw_ref, wout_ref, wgu_in_ref, wd_in_ref,
                 o_ref, st_ref, wgu_out_ref, wd_out_ref, h_ref, ubuf_ref, *, tt):
    x = x_ref[...]
    h_ref[...] = _rms(x, g_ref[...]).astype(BF16)
    _carry_halo(ubuf_ref, A_HALO, tt)
    cg = _dot(h_ref[...], win_ref[:, D_MODEL:2 * D_MODEL])
    hv = _dot(h_ref[...], win_ref[:, 2 * D_MODEL:3 * D_MODEL])
    ubuf_ref[A_HALO:A_HALO + tt, :] = cg * hv
    y = cw_ref[A_WIDTH - 1:A_WIDTH, :] * ubuf_ref[A_HALO:A_HALO + tt, :]
    for k in range(A_WIDTH - 1):
        off = A_HALO - (A_WIDTH - 1) + k
        y = y + cw_ref[k:k + 1, :] * ubuf_ref[off:off + tt, :]
    bg = _dot(h_ref[...], win_ref[:, 0:D_MODEL])
    o_ref[...] = x + _dot((bg * y).astype(BF16), wout_ref[...])
    _stage(wgu_in_ref, wd_in_ref, wgu_out_ref, wd_out_ref)

    @pl.when(pl.program_id(1) == pl.num_programs(1) - 1)
    def _():
        st_ref[...] = ubuf_ref[A_HALO + tt - (A_WIDTH - 1):A_HALO + tt, :]


def _mixa_prompt(x, g, win, cw, wout, wgu, wd, layer):
    n, t, _ = x.shape
    tt = A_TOK_TILE
    st_in, st_out, st_shape = _stage_specs(layer, n, t // tt)
    return pl.pallas_call(
        functools.partial(_mixa_kernel, tt=tt),
        out_shape=(jax.ShapeDtypeStruct((n, t, D_MODEL), F32),
                   jax.ShapeDtypeStruct((n, A_WIDTH - 1, D_MODEL), F32)) + st_shape,
        grid=(n, t // tt),
        in_specs=[pl.BlockSpec((None, tt, D_MODEL), lambda i, j: (i, j, 0)),
                  _const_spec((1, D_MODEL)),
                  _const_spec((D_MODEL, 3 * D_MODEL)),
                  _const_spec((A_WIDTH, D_MODEL)),
                  _const_spec((D_MODEL, D_MODEL))] + st_in,
        out_specs=(pl.BlockSpec((None, tt, D_MODEL), lambda i, j: (i, j, 0)),
                   pl.BlockSpec((None, A_WIDTH - 1, D_MODEL), lambda i, j: (i, 0, 0))) + st_out,
        scratch_shapes=[pltpu.VMEM((tt, D_MODEL), BF16),
                        pltpu.VMEM((A_HALO + tt, D_MODEL), F32)],
        compiler_params=_params("arbitrary", "arbitrary"),
        name="mix_shortconv",
    )(x, g, win, cw, wout, wgu, wd)


def _layer_norm(y, g, b):
    mu = jnp.mean(y, axis=-1, keepdims=True)
    yc = y - mu
    var = jnp.mean(yc * yc, axis=-1, keepdims=True)
    return yc * lax.rsqrt(var + LN_EPS) * g + b


def _dwconv_tile(ubuf_ref, taps_ref, y_ref, r0):
    sub = V7X_SUBLANES
    nblk = CONV_ROWS // sub
    na = -(-B_WIDTH // sub)
    assert B_HALO == na * sub
    rowid = lax.broadcasted_iota(jnp.int32, (sub, V7X_LANES), 0)
    for lt in range(D_MODEL // V7X_LANES):
        lanes = slice(lt * V7X_LANES, (lt + 1) * V7X_LANES)
        ub = [ubuf_ref[pl.ds(r0 + sub * i, sub), lanes] for i in range(nblk + na)]
        acc = None
        for r in range(sub):
            z = []
            for m in range(nblk + 1):
                zm = None
                for a, s in enumerate(range(r, B_WIDTH, sub)):
                    term = taps_ref[s, :, lanes] * ub[na - 1 + m - a]
                    zm = term if zm is None else zm + term
                z.append(zm)
            if r == 0:
                shifted = z[1:]
            else:
                rolled = [pltpu.roll(zm, r, 0) for zm in z]
                shifted = [jnp.where(rowid >= r, rolled[n + 1], rolled[n]) for n in range(nblk)]
            acc = shifted if acc is None else [p + c for p, c in zip(acc, shifted)]
        for n in range(nblk):
            y_ref[pl.ds(r0 + sub * n, sub), lanes] = acc[n]


def _mixb_kernel(x_ref, g_ref, w1_ref, b1_ref, taps_ref, dwb_ref, lng_ref, lnb_ref, w2_ref, b2_ref,
                 wgu_in_ref, wd_in_ref, o_ref, st_ref, wgu_out_ref, wd_out_ref,
                 h_ref, ubuf_ref, y_ref, *, tt):
    x = x_ref[...]
    h_ref[...] = _rms(x, g_ref[...]).astype(BF16)
    _carry_halo(ubuf_ref, B_HALO, tt)
    a = _dot(h_ref[...], w1_ref[:, 0:D_MODEL]) + b1_ref[:, 0:D_MODEL]
    gate = _dot(h_ref[...], w1_ref[:, D_MODEL:2 * D_MODEL]) + b1_ref[:, D_MODEL:2 * D_MODEL]
    ubuf_ref[B_HALO:B_HALO + tt, :] = a * jax.nn.sigmoid(gate)

    def conv_body(i, carry):
        _dwconv_tile(ubuf_ref, taps_ref, y_ref, pl.multiple_of(i * CONV_ROWS, CONV_ROWS))
        return carry

    lax.fori_loop(0, tt // CONV_ROWS, conv_body, 0)
    z = _silu(_layer_norm(y_ref[...] + dwb_ref[...], lng_ref[...], lnb_ref[...]))
    o_ref[...] = x + _dot(z.astype(BF16), w2_ref[...]) + b2_ref[...]
    _stage(wgu_in_ref, wd_in_ref, wgu_out_ref, wd_out_ref)

    @pl.when(pl.program_id(1) == pl.num_programs(1) - 1)
    def _():
        st_ref[...] = ubuf_ref[B_HALO + tt - (B_WIDTH - 1):B_HALO + tt, :]


def _mixb_prompt(x, g, w1, b1, taps, dwb, lng, lnb, w2, b2, wgu, wd, layer):
    n, t, _ = x.shape
    tt = TOK_TILE
    st_in, st_out, st_shape = _stage_specs(layer, n, t // tt)
    return pl.pallas_call(
        functools.partial(_mixb_kernel, tt=tt),
        out_shape=(jax.ShapeDtypeStruct((n, t, D_MODEL), F32),
                   jax.ShapeDtypeStruct((n, B_WIDTH - 1, D_MODEL), F32)) + st_shape,
        grid=(n, t // tt),
        in_specs=[pl.BlockSpec((None, tt, D_MODEL), lambda i, j: (i, j, 0)),
                  _const_spec((1, D_MODEL)),
                  _const_spec((D_MODEL, 2 * D_MODEL)),
                  _const_spec((1, 2 * D_MODEL)),
                  _const_spec((B_WIDTH, V7X_SUBLANES, D_MODEL)),
                  _const_spec((1, D_MODEL)),
                  _const_spec((1, D_MODEL)),
                  _const_spec((1, D_MODEL)),
                  _const_spec((D_MODEL, D_MODEL)),
                  _const_spec((1, D_MODEL))] + st_in,
        out_specs=(pl.BlockSpec((None, tt, D_MODEL), lambda i, j: (i, j, 0)),
                   pl.BlockSpec((None, B_WIDTH - 1, D_MODEL), lambda i, j: (i, 0, 0))) + st_out,
        scratch_shapes=[pltpu.VMEM((tt, D_MODEL), BF16),
                        pltpu.VMEM((B_HALO + tt, D_MODEL), F32),
                        pltpu.VMEM((tt, D_MODEL), F32)],
        compiler_params=_params("arbitrary", "arbitrary"),
        name="mix_conformer",
    )(x, g, w1, b1, taps, dwb, lng, lnb, w2, b2, wgu, wd)


def _hgrn_constants():
    c = HG_CHUNK
    t = np.arange(c)[:, None]
    r = np.arange(c)[None, :]
    blocks = [(r <= t), (r > t)]
    masks = [(t == r)]
    for lvl in range(HG_LEVELS):
        h = 1 << lvl
        m = (t // (2 * h)) * (2 * h) + h - 1
        upper = (t & h) != 0
        blocks.append(np.where(upper, (r > m) & (r <= t), (r > t) & (r <= m)))
        masks.append((((t ^ r) >> lvl) == 1) & upper)
    a_stack = np.concatenate(blocks, axis=0).astype(np.float32)
    a_stack = np.concatenate([a_stack, a_stack], axis=1)
    m_stack = np.stack(masks, axis=0).astype(np.float32)
    return jnp.asarray(a_stack, BF16), jnp.asarray(m_stack, BF16)


def _gates(fx, lb):
    sig = jax.nn.sigmoid(fx)
    return lb + (1.0 - lb) * sig, (1.0 - lb) * (1.0 - sig)


def _head_out(o, gx, gn):
    o = o * lax.rsqrt(jnp.mean(o * o, axis=-1, keepdims=True) + NORM_EPS) * gn
    return o * _silu(gx)


def _hgrn_head_chunk(q, kk, v, s, e, m_ref):
    c = HG_CHUNK
    e_b = e(0)
    vb = v.astype(BF16)
    o = _dot((q * e_b).astype(BF16), s.astype(BF16))
    scores = _dot(q.astype(BF16), kk.T.astype(BF16)).astype(BF16) * m_ref[0]
    for lvl in range(HG_LEVELS):
        e_l = e(2 + lvl)
        p_l = _dot((q * e_l).astype(BF16), (kk * e_l).T.astype(BF16))
        scores = scores + p_l.astype(BF16) * m_ref[1 + lvl]
    o = o + _dot(scores, vb)
    decay = jnp.broadcast_to(e_b[c - 1:c, :], (C_DK, C_DK)).T
    s_new = s * decay + _dot_tn((kk * e(1)).astype(BF16), vb)
    return o, s_new


def _mixc_kernel(x_ref, g_ref, wq_ref, lb_ref, gn_ref, wout_ref, a_ref, m_ref, wgu_in_ref, wd_in_ref,
                 o_ref, sout_ref, wgu_out_ref, wd_out_ref,
                 h_ref, p_ref, qk_ref, e_ref, om_ref, s_ref, *, tt):
    j = pl.program_id(1)
    d = D_MODEL
    c = HG_CHUNK

    @pl.when(j == 0)
    def _():
        s_ref[...] = jnp.zeros_like(s_ref)

    x = x_ref[...]
    h_ref[...] = _rms(x, g_ref[...]).astype(BF16)
    nb = 4 * V7X_LANES
    for b in range(4 * d // nb):
        p_ref[:, b * nb:(b + 1) * nb] = _dot(h_ref[...], wq_ref[:, b * nb:(b + 1) * nb])

    def prep(ci):
        rows = slice(ci * c, (ci + 1) * c)
        f, kk = _gates(p_ref[rows, d:2 * d], lb_ref[...])
        log2f = jnp.log2(f)
        hi = log2f.astype(BF16)
        lo = (log2f - hi.astype(F32)).astype(BF16)
        e_ref[ci % 2] = jnp.exp2(_dot(a_ref[...], jnp.concatenate([hi, lo], axis=0)))
        qk_ref[ci % 2, :, 0:d] = _silu(p_ref[rows, 0:d]) * (C_DK ** -0.5)
        qk_ref[ci % 2, :, d:2 * d] = kk

    def heads(ci):
        rows = slice(ci * c, (ci + 1) * c)
        slot = ci % 2
        for hd in range(C_HEADS):
            lanes = slice(hd * C_DK, (hd + 1) * C_DK)
            o, s_ref[hd] = _hgrn_head_chunk(
                qk_ref[slot, :, lanes], qk_ref[slot, :, d + hd * C_DK:d + (hd + 1) * C_DK],
                p_ref[rows, 2 * d + hd * C_DV:2 * d + (hd + 1) * C_DV], s_ref[hd],
                lambda r: e_ref[slot, r * c:(r + 1) * c, lanes], m_ref)
            gx = p_ref[rows, 3 * d + hd * C_DV:3 * d + (hd + 1) * C_DV]
            om_ref[rows, lanes] = _head_out(o, gx, gn_ref[...]).astype(BF16)

    nchunk = tt // c
    prep(0)
    for ci in range(nchunk):
        if ci + 1 < nchunk:
            prep(ci + 1)
        heads(ci)
    o_ref[...] = x + _dot(om_ref[...], wout_ref[...])
    _stage(wgu_in_ref, wd_in_ref, wgu_out_ref, wd_out_ref)

    @pl.when(j == pl.num_programs(1) - 1)
    def _():
        sout_ref[...] = s_ref[...]


def _mixc_prompt(x, g, wq, lb_row, gn, wout, a_stack, m_stack, wgu, wd, layer):
    n, t, _ = x.shape
    tt = TOK_TILE
    st_in, st_out, st_shape = _stage_specs(layer, n, t // tt)
    return pl.pallas_call(
        functools.partial(_mixc_kernel, tt=tt),
        out_shape=(jax.ShapeDtypeStruct((n, t, D_MODEL), F32),
                   jax.ShapeDtypeStruct((n, C_HEADS, C_DK, C_DV), F32)) + st_shape,
        grid=(n, t // tt),
        in_specs=[pl.BlockSpec((None, tt, D_MODEL), lambda i, j: (i, j, 0)),
                  _const_spec((1, D_MODEL)),
                  _const_spec((D_MODEL, 4 * D_MODEL)),
                  _const_spec((1, D_MODEL)),
                  _const_spec((1, C_DV)),
                  _const_spec((D_MODEL, D_MODEL)),
                  _const_spec(a_stack.shape),
                  _const_spec(m_stack.shape)] + st_in,
        out_specs=(pl.BlockSpec((None, tt, D_MODEL), lambda i, j: (i, j, 0)),
                   pl.BlockSpec((None, C_HEADS, C_DK, C_DV), lambda i, j: (i, 0, 0, 0))) + st_out,
        scratch_shapes=[pltpu.VMEM((tt, D_MODEL), BF16),
                        pltpu.VMEM((tt, 4 * D_MODEL), F32),
                        pltpu.VMEM((2, HG_CHUNK, 2 * D_MODEL), F32),
                        pltpu.VMEM((2, (2 + HG_LEVELS) * HG_CHUNK, D_MODEL), F32),
                        pltpu.VMEM((tt, D_MODEL), BF16),
                        pltpu.VMEM((C_HEADS, C_DK, C_DV), F32)],
        compiler_params=_params("arbitrary", "arbitrary"),
        name="mix_hgrn2",
    )(x, g, wq, lb_row, gn, wout, a_stack, m_stack, wgu, wd)


def _mixa_sample_kernel(x_ref, st_ref, g_ref, win_ref, cw_ref, wout_ref, o_ref, sto_ref):
    x = x_ref[...]
    h = _rms(x, g_ref[...]).astype(BF16)
    p = _dot(h, win_ref[...])
    bg = p[:, 0:D_MODEL]
    u = p[:, D_MODEL:2 * D_MODEL] * p[:, 2 * D_MODEL:3 * D_MODEL]
    y = cw_ref[0:1, :] * st_ref[0] + cw_ref[1:2, :] * st_ref[1] + cw_ref[2:3, :] * u
    o_ref[...] = x + _dot((bg * y).astype(BF16), wout_ref[...])
    sto_ref[0] = st_ref[1]
    sto_ref[1] = u


def _mixa_sample(x2d, st, g, win, cw, wout):
    n = x2d.shape[0]
    return pl.pallas_call(
        _mixa_sample_kernel,
        out_shape=(jax.ShapeDtypeStruct((n, D_MODEL), F32),
                   jax.ShapeDtypeStruct((A_WIDTH - 1, n, D_MODEL), F32)),
        grid=(1,),
        in_specs=[_const_spec((n, D_MODEL)),
                  _const_spec((A_WIDTH - 1, n, D_MODEL)),
                  _const_spec((1, D_MODEL)),
                  _const_spec((D_MODEL, 3 * D_MODEL)),
                  _const_spec((A_WIDTH, D_MODEL)),
                  _const_spec((D_MODEL, D_MODEL))],
        out_specs=(pl.BlockSpec((n, D_MODEL), lambda i: (0, 0)),
                   pl.BlockSpec((A_WIDTH - 1, n, D_MODEL), lambda i: (0, 0, 0))),
        compiler_params=_params("arbitrary"),
        name="mix_shortconv_sample",
    )(x2d, st, g, win, cw, wout)


def _mixb_sample_kernel(x_ref, st_ref, g_ref, w1_ref, b1_ref, dw_ref, dwb_ref, lng_ref, lnb_ref,
                        w2_ref, b2_ref, o_ref, sto_ref, u_ref, y_ref, *, nb):
    i = pl.program_id(0)
    hist = B_WIDTH - 1

    @pl.when(i == 0)
    def _():
        h = _rms(x_ref[...], g_ref[...]).astype(BF16)
        p = _dot(h, w1_ref[...]) + b1_ref[...]
        u_ref[...] = p[:, 0:D_MODEL] * jax.nn.sigmoid(p[:, D_MODEL:2 * D_MODEL])

    rows = pl.ds(pl.multiple_of(i * nb, nb), nb)
    u = u_ref[rows, :]
    y = dw_ref[hist:hist + 1, :] * u
    for k in range(hist):
        y = y + dw_ref[k:k + 1, :] * st_ref[k]
        sto_ref[k] = st_ref[k + 1] if k + 1 < hist else u
    y_ref[rows, :] = y

    @pl.when(i == pl.num_programs(0) - 1)
    def _():
        z = _silu(_layer_norm(y_ref[...] + dwb_ref[...], lng_ref[...], lnb_ref[...]))
        o_ref[...] = x_ref[...] + _dot(z.astype(BF16), w2_ref[...]) + b2_ref[...]


def _mixb_sample(x2d, st, layer, g, w1, b1, dw, dwb, lng, lnb, w2, b2):
    n = x2d.shape[0]
    nb = SAMPLE_BLOCK_B
    hist = B_WIDTH - 1
    return pl.pallas_call(
        functools.partial(_mixb_sample_kernel, nb=nb),
        out_shape=(jax.ShapeDtypeStruct((n, D_MODEL), F32),
                   jax.ShapeDtypeStruct((hist, n, D_MODEL), F32)),
        grid=(n // nb,),
        in_specs=[_const_spec((n, D_MODEL)),
                  pl.BlockSpec((None, hist, nb, D_MODEL), lambda i: (layer, 0, i, 0)),
                  _const_spec((1, D_MODEL)),
                  _const_spec((D_MODEL, 2 * D_MODEL)),
                  _const_spec((1, 2 * D_MODEL)),
                  _const_spec((B_WIDTH, D_MODEL)),
                  _const_spec((1, D_MODEL)),
                  _const_spec((1, D_MODEL)),
                  _const_spec((1, D_MODEL)),
                  _const_spec((D_MODEL, D_MODEL)),
                  _const_spec((1, D_MODEL))],
        out_specs=(pl.BlockSpec((n, D_MODEL), lambda i: (0, 0)),
                   pl.BlockSpec((hist, nb, D_MODEL), lambda i: (0, i, 0))),
        scratch_shapes=[pltpu.VMEM((n, D_MODEL), F32),
                        pltpu.VMEM((n, D_MODEL), F32)],
        compiler_params=_params("arbitrary"),
        name="mix_conformer_sample",
    )(x2d, st, g, w1, b1, dw, dwb, lng, lnb, w2, b2)


def _sample_selector():
    nb = SAMPLE_BLOCK_C
    sel = np.zeros((4 * nb, nb * C_DV), np.float32)
    for p in range(3):
        for r in range(nb):
            sel[p * nb + r, r * C_DV:(r + 1) * C_DV] = 1.0
    return jnp.asarray(sel, BF16)


def _split3_rows(a):
    hi = a.astype(BF16).astype(F32)
    r1 = a - hi
    mid = r1.astype(BF16).astype(F32)
    lo = r1 - mid
    return jnp.concatenate([hi, mid, lo, jnp.zeros_like(a)], axis=0).astype(BF16)


def _mixc_sample_kernel(x_ref, s_ref, g_ref, wq_ref, lb_ref, gn_ref, wout_ref, sel_ref,
                        o_ref, so_ref, p_ref, orow_ref, om_ref, *, nb):
    i = pl.program_id(0)

    @pl.when(i == 0)
    def _():
        h = _rms(x_ref[...], g_ref[...]).astype(BF16)
        p_ref[...] = _dot(h, wq_ref[...])

    rows = pl.ds(pl.multiple_of(i * nb, nb), nb)
    d = D_MODEL
    f, kk = _gates(p_ref[rows, d:2 * d], lb_ref[...])
    q = _silu(p_ref[rows, 0:d]) * (C_DK ** -0.5)
    v = p_ref[rows, 2 * d:3 * d]

    for hd in range(C_HEADS):
        lanes = slice(hd * C_DK, (hd + 1) * C_DK)
        parts = jnp.concatenate([_split3_rows(a[:, lanes]) for a in (f, kk, q)], axis=1)
        cols = _dot_tn(parts, sel_ref[...])
        for r in range(nb):
            blk = slice(r * C_DV, (r + 1) * C_DV)
            s_new = cols[0:C_DK, blk] * s_ref[r, hd] + cols[C_DK:2 * C_DK, blk] * v[r:r + 1, lanes]
            so_ref[r, hd] = s_new
            orow_ref[r:r + 1, lanes] = jnp.sum(cols[2 * C_DK:3 * C_DK, blk] * s_new, axis=0,
                                               keepdims=True)

    gx = p_ref[rows, 3 * d:4 * d]
    for hd in range(C_HEADS):
        lanes = slice(hd * C_DK, (hd + 1) * C_DK)
        om_ref[rows, lanes] = _head_out(orow_ref[:, lanes], gx[:, lanes], gn_ref[...])

    @pl.when(i == pl.num_programs(0) - 1)
    def _():
        o_ref[...] = x_ref[...] + _dot(om_ref[...].astype(BF16), wout_ref[...])


def _mixc_sample(x2d, s, g, wq, lb_row, gn, wout, sel):
    n = x2d.shape[0]
    nb = SAMPLE_BLOCK_C
    return pl.pallas_call(
        functools.partial(_mixc_sample_kernel, nb=nb),
        out_shape=(jax.ShapeDtypeStruct((n, D_MODEL), F32),
                   jax.ShapeDtypeStruct((n, C_HEADS, C_DK, C_DV), F32)),
        grid=(n // nb,),
        in_specs=[_const_spec((n, D_MODEL)),
                  pl.BlockSpec((nb, C_HEADS, C_DK, C_DV), lambda i: (i, 0, 0, 0)),
                  _const_spec((1, D_MODEL)),
                  _const_spec((D_MODEL, 4 * D_MODEL)),
                  _const_spec((1, D_MODEL)),
                  _const_spec((1, C_DV)),
                  _const_spec((D_MODEL, D_MODEL)),
                  _const_spec(sel.shape)],
        out_specs=(pl.BlockSpec((n, D_MODEL), lambda i: (0, 0)),
                   pl.BlockSpec((nb, C_HEADS, C_DK, C_DV), lambda i: (i, 0, 0, 0))),
        scratch_shapes=[pltpu.VMEM((n, 4 * D_MODEL), F32),
                        pltpu.VMEM((nb, D_MODEL), F32),
                        pltpu.VMEM((n, D_MODEL), F32)],
        compiler_params=_params("arbitrary"),
        name="mix_hgrn2_sample",
    )(x2d, s, g, wq, lb_row, gn, wout, sel)


def _row(a):
    return a.reshape(1, -1)


def kernel(x_prompt, x_sample, state_conva, state_convb, state_hgrn, norm_mix, a_w_in, a_conv_w, a_w_out, b_w_pw1, b_b_pw1, b_dw_w, b_dw_b, b_ln_g, b_ln_b, b_w_pw2, b_b_pw2, c_lower_bounds, c_w_qfig, c_gnorm, c_w_out, norm_ffn, ffn_w_gate_up, ffn_w_down, norm_final):
    nb, t, d = x_prompt.shape
    ns = x_sample.shape[0]
    bf = lambda w: w.astype(BF16)
    a_w_in, a_w_out, b_w_pw1, b_w_pw2 = bf(a_w_in), bf(a_w_out), bf(b_w_pw1), bf(b_w_pw2)
    c_w_qfig, c_w_out = bf(c_w_qfig), bf(c_w_out)

    sm = jax.nn.softmax(c_lower_bounds.astype(F32), axis=0)
    lower = jnp.cumsum(sm, axis=0) - sm[0]
    a_stack, m_stack = _hgrn_constants()
    gfin = _row(norm_final)

    xp = x_prompt
    xs = x_sample.reshape(ns, d)
    g_ffn = norm_ffn.reshape(DEPTH, 1, d)
    conva_p, conva_s, convb_p, convb_s, hgrn_p, hgrn_s = [], [], [], [], [], []
    for i in range(DEPTH):
        kind, j = i % 3, i // 3
        g = _row(norm_mix[i])
        if kind == 0:
            w = (g, a_w_in[j], a_conv_w[j], a_w_out[j])
            xp, st, wgu, wd = _mixa_prompt(xp, *w, ffn_w_gate_up, ffn_w_down, i)
            conva_p.append(st)
            xs, st = _mixa_sample(xs, jnp.swapaxes(state_conva[j], 0, 1), *w)
            conva_s.append(jnp.swapaxes(st, 0, 1))
        elif kind == 1:
            w = (b_w_pw1[j], _row(b_b_pw1[j]))
            w2 = (_row(b_dw_b[j]), _row(b_ln_g[j]), _row(b_ln_b[j]), b_w_pw2[j], _row(b_b_pw2[j]))
            taps = jnp.broadcast_to(b_dw_w[j][::-1][:, None, :], (B_WIDTH, V7X_SUBLANES, d))
            xp, st, wgu, wd = _mixb_prompt(xp, g, *w, taps, *w2, ffn_w_gate_up, ffn_w_down, i)
            convb_p.append(st)
            xs, st = _mixb_sample(xs, jnp.swapaxes(state_convb, 1, 2), j, g, *w, b_dw_w[j], *w2)
            convb_s.append(jnp.swapaxes(st, 0, 1))
        else:
            gn = _row(c_gnorm[j])
            xp, st, wgu, wd = _mixc_prompt(xp, g, c_w_qfig[j], _row(lower[i]), gn, c_w_out[j], a_stack,
                                           m_stack, ffn_w_gate_up, ffn_w_down, i)
            hgrn_p.append(st)
            xs, st = _mixc_sample(xs, state_hgrn[j], g, c_w_qfig[j], _row(lower[i]), gn, c_w_out[j],
                                  _sample_selector())
            hgrn_s.append(st)
        xp, xs = _ffn(xp.reshape(nb * t, d), xs, g_ffn, wgu, wd, gfin, layer=i, final=i == DEPTH - 1)
        xp = xp.reshape(nb, t, d)
    stack = lambda xs_: xs_[0][None] if len(xs_) == 1 else jnp.stack(xs_)
    return (xp, xs.reshape(ns, 1, d), stack(conva_p), stack(conva_s), stack(convb_p),
            stack(convb_s), stack(hgrn_p), stack(hgrn_s))
```
